```python
import math
import jax, jax.numpy as jnp
from jax import lax
import numpy as np

D_MODEL = 1024
BATCH = 16
SEQ = 2048
DEPTH = 1

D_MIX = D_MODEL
D_SSM = D_MIX // 2
D_ATTN = D_MIX - D_SSM
SSM_GROUP = 16
N_SSM_GROUPS = D_SSM // SSM_GROUP
SSM_STATE = 64
DT_MIN = 1e-3
DT_MAX = 1e-1
HEAD_DIM = 64
N_HEADS = D_ATTN // HEAD_DIM
MOBA_BLOCK = 256
MOBA_TOPK = 3
Q_CHUNK = 16
D_FF = -(-8 * D_MODEL // (3 * 256)) * 256
RMS_EPS = 1e-6

kernel_name = "hybrid_s5_moba_block"


def rmsnorm(x, g):
    xf = x.astype(jnp.float32)
    y = xf * lax.rsqrt(jnp.mean(xf * xf, axis=-1, keepdims=True) + RMS_EPS)
    return (y * g.astype(jnp.float32)).astype(x.dtype)


def _complex_affine_combine(e1, e2):
    a1r, a1i, b1r, b1i = e1
    a2r, a2i, b2r, b2i = e2
    ar = a2r * a1r - a2i * a1i
    ai = a2r * a1i + a2i * a1r
    br = a2r * b1r - a2i * b1i + b2r
    bi = a2r * b1i + a2i * b1r + b2i
    return (ar, ai, br, bi)


def s5_mixer(u, a_re, a_im, log_dt, b_re, b_im, c_re, c_im, d_skip, w_glu, b_glu):
    bsz, s, _ = u.shape
    f32 = jnp.float32
    uf = u.astype(f32).reshape(bsz, s, N_SSM_GROUPS, SSM_GROUP)
    a_re = a_re.astype(f32); a_im = a_im.astype(f32)
    dt = jnp.exp(log_dt.astype(f32))[:, None]
    mag = jnp.exp(a_re * dt)
    ang = a_im * dt
    lb_re = mag * jnp.cos(ang)
    lb_im = mag * jnp.sin(ang)
    nr = lb_re - 1.0
    den = a_re * a_re + a_im * a_im
    cr = (nr * a_re + lb_im * a_im) / den
    ci = (lb_im * a_re - nr * a_im) / den
    b_re = b_re.astype(f32); b_im = b_im.astype(f32)
    bb_re = cr[..., None] * b_re - ci[..., None] * b_im
    bb_im = cr[..., None] * b_im + ci[..., None] * b_re
    bu_re = jnp.einsum('bsgh,gph->bsgp', uf, bb_re)
    bu_im = jnp.einsum('bsgh,gph->bsgp', uf, bb_im)
    ar = jnp.broadcast_to(lb_re, (1, s) + lb_re.shape)
    ai = jnp.broadcast_to(lb_im, (1, s) + lb_im.shape)
    _, _, st_re, st_im = lax.associative_scan(
        _complex_affine_combine, (ar, ai, bu_re, bu_im), axis=1)
    y = (jnp.einsum('bsgp,ghp->bsgh', st_re, c_re.astype(f32))
         - jnp.einsum('bsgp,ghp->bsgh', st_im, c_im.astype(f32))
         + d_skip.astype(f32) * uf)
    y = jax.nn.gelu(y.reshape(bsz, s, D_SSM))
    y = y * jax.nn.sigmoid(jnp.einsum('bsc,ce->bse', y, w_glu.astype(f32)) + b_glu.astype(f32))
    return y.astype(u.dtype)


def moba_attention(q, k, v):
    bsz, s, _ = q.shape
    n_blocks = -(-s // MOBA_BLOCK)
    sp = n_blocks * MOBA_BLOCK
    k_sel = min(MOBA_TOPK, n_blocks)
    n_chunks = sp // Q_CHUNK

    def heads(t):
        t = t.astype(jnp.float32).reshape(bsz, s, N_HEADS, HEAD_DIM).transpose(0, 2, 1, 3)
        return jnp.pad(t, ((0, 0), (0, 0), (0, sp - s), (0, 0)))

    qh = heads(q) * (HEAD_DIM ** -0.5)
    kh = heads(k)
    vh = heads(v)
    k_blocks = kh.reshape(bsz, N_HEADS, n_blocks, MOBA_BLOCK, HEAD_DIM)
    v_blocks = vh.reshape(bsz, N_HEADS, n_blocks, MOBA_BLOCK, HEAD_DIM)
    k_mean = jnp.mean(k_blocks, axis=3)

    gate = jnp.einsum('bhsd,bhnd->bhsn', qh, k_mean)
    q_blk = jnp.arange(sp) // MOBA_BLOCK
    past = jnp.arange(n_blocks)[None, :] < q_blk[:, None]
    gate = jnp.where(past, gate, -jnp.inf)
    _, sel_idx = lax.top_k(gate, k_sel)
    sel_valid = jnp.arange(k_sel)[None, :] < q_blk[:, None]

    def to_chunks(t):
        return jnp.moveaxis(t.reshape(bsz, N_HEADS, n_chunks, Q_CHUNK, *t.shape[3:]), 2, 0)

    q_c = to_chunks(qh)
    idx_c = to_chunks(sel_idx)
    valid_c = sel_valid.reshape(n_chunks, Q_CHUNK, k_sel)
    gather = jax.vmap(jax.vmap(lambda blocks, ix: blocks[ix]))

    def chunk_attend(args):
        c, qc, ic, vc = args
        kg = gather(k_blocks, ic)
        vg = gather(v_blocks, ic)
        s_sel = jnp.einsum('bhqd,bhqnjd->bhqnj', qc, kg)
        s_sel = jnp.where(vc[None, None, :, :, None], s_sel, -jnp.inf)
        own = (c * Q_CHUNK) // MOBA_BLOCK
        k_own = lax.dynamic_index_in_dim(k_blocks, own, axis=2, keepdims=False)
        v_own = lax.dynamic_index_in_dim(v_blocks, own, axis=2, keepdims=False)
        s_own = jnp.einsum('bhqd,bhjd->bhqj', qc, k_own)
        q_pos = c * Q_CHUNK + jnp.arange(Q_CHUNK)
        k_pos = own * MOBA_BLOCK + jnp.arange(MOBA_BLOCK)
        s_own = jnp.where(k_pos[None, :] <= q_pos[:, None], s_own, -jnp.inf)
        scores = jnp.concatenate(
            [s_sel.reshape(bsz, N_HEADS, Q_CHUNK, k_sel * MOBA_BLOCK), s_own], axis=-1)
        p = jax.nn.softmax(scores, axis=-1)
        p_sel = p[..., :k_sel * MOBA_BLOCK].reshape(bsz, N_HEADS, Q_CHUNK, k_sel, MOBA_BLOCK)
        p_own = p[..., k_sel * MOBA_BLOCK:]
        return (jnp.einsum('bhqnj,bhqnjd->bhqd', p_sel, vg)
                + jnp.einsum('bhqj,bhjd->bhqd', p_own, v_own))

    out = lax.map(chunk_attend, (jnp.arange(n_chunks), q_c, idx_c, valid_c))
    out = out.transpose(1, 0, 3, 2, 4).reshape(bsz, sp, D_ATTN)
    return out[:, :s].astype(q.dtype)


def hybrid_layer(x, g_pre_mix, w_in, ssm_a_re, ssm_a_im, ssm_log_dt, ssm_b_re, ssm_b_im,
                 ssm_c_re, ssm_c_im, ssm_d, w_glu, b_glu, g_ssm_out, g_attn_out, w_out,
                 g_post_mix, g_pre_ffn, w_gate, w_up, w_down, g_post_ffn):
    h = rmsnorm(x, g_pre_mix)
    z = jnp.einsum('bsd,de->bse', h, w_in)
    u, q, k, v = jnp.split(z, [D_SSM, D_SSM + D_ATTN, D_SSM + 2 * D_ATTN], axis=-1)
    y_ssm = s5_mixer(u, ssm_a_re, ssm_a_im, ssm_log_dt, ssm_b_re, ssm_b_im,
                     ssm_c_re, ssm_c_im, ssm_d, w_glu, b_glu)
    y_attn = moba_attention(q, k, v)
    mix = jnp.concatenate([rmsnorm(y_ssm, g_ssm_out), rmsnorm(y_attn, g_attn_out)], axis=-1)
    x = x + rmsnorm(jnp.einsum('bse,ed->bsd', mix, w_out), g_post_mix)
    h = rmsnorm(x, g_pre_ffn)
    f = jax.nn.silu(jnp.einsum('bsd,df->bsf', h, w_gate)) * jnp.einsum('bsd,df->bsf', h, w_up)
    x = x + rmsnorm(jnp.einsum('bsf,fd->bsd', f, w_down), g_post_ffn)
    return x


def setup_inputs(seed: int = 0) -> dict:
    key = jax.random.key(seed)
    ks = jax.random.split(key, 24)
    f32 = jnp.float32
    G, P, H = N_SSM_GROUPS, SSM_STATE, SSM_GROUP
    L = DEPTH

    def nrm(k, shape, scale):
        return jax.random.normal(k, shape, f32) * scale

    def gain(k, n):
        return 1.0 + 0.01 * jax.random.normal(k, (L, n), f32)

    n_idx = jnp.arange(P, dtype=f32)
    return dict(
        x=nrm(ks[0], (BATCH, SEQ, D_MODEL), 1.0),
        g_pre_mix=gain(ks[1], D_MODEL),
        w_in=nrm(ks[2], (L, D_MODEL, D_SSM + 3 * D_ATTN), D_MODEL ** -0.5),
        ssm_a_re=-0.5 + nrm(ks[3], (L, G, P), 0.01),
        ssm_a_im=math.pi * n_idx[None, None, :] + nrm(ks[4], (L, G, P), 0.01),
        ssm_log_dt=jax.random.uniform(ks[5], (L, G), f32, minval=math.log(DT_MIN), maxval=math.log(DT_MAX)),
        ssm_b_re=nrm(ks[6], (L, G, P, H), (2 * H) ** -0.5),
        ssm_b_im=nrm(ks[7], (L, G, P, H), (2 * H) ** -0.5),
        ssm_c_re=nrm(ks[8], (L, G, H, P), (2 * P) ** -0.5),
        ssm_c_im=nrm(ks[9], (L, G, H, P), (2 * P) ** -0.5),
        ssm_d=nrm(ks[10], (L, G, H), 1.0),
        w_glu=nrm(ks[11], (L, D_SSM, D_SSM), D_SSM ** -0.5),
        b_glu=nrm(ks[12], (L, D_SSM), 0.01),
        g_ssm_out=gain(ks[13], D_SSM),
        g_attn_out=gain(ks[14], D_ATTN),
        w_out=nrm(ks[15], (L, D_MIX, D_MODEL), D_MIX ** -0.5),
        g_post_mix=gain(ks[16], D_MODEL),
        g_pre_ffn=gain(ks[17], D_MODEL),
        w_gate=nrm(ks[18], (L, D_MODEL, D_FF), D_MODEL ** -0.5),
        w_up=nrm(ks[19], (L, D_MODEL, D_FF), D_MODEL ** -0.5),
        w_down=nrm(ks[20], (L, D_FF, D_MODEL), D_FF ** -0.5),
        g_post_ffn=gain(ks[21], D_MODEL),
    )


def reference(x, g_pre_mix, w_in, ssm_a_re, ssm_a_im, ssm_log_dt, ssm_b_re, ssm_b_im,
              ssm_c_re, ssm_c_im, ssm_d, w_glu, b_glu, g_ssm_out, g_attn_out, w_out,
              g_post_mix, g_pre_ffn, w_gate, w_up, w_down, g_post_ffn):
    for l in range(DEPTH):
        x = hybrid_layer(x, g_pre_mix[l], w_in[l], ssm_a_re[l], ssm_a_im[l], ssm_log_dt[l],
                         ssm_b_re[l], ssm_b_im[l], ssm_c_re[l], ssm_c_im[l], ssm_d[l],
                         w_glu[l], b_glu[l], g_ssm_out[l], g_attn_out[l], w_out[l],
                         g_post_mix[l], g_pre_ffn[l], w_gate[l], w_up[l], w_down[l], g_post_ffn[l])
    return x
```

```python
import functools

import jax
import jax.numpy as jnp
from jax import lax
from jax.experimental import pallas as pl
from jax.experimental.pallas import tpu as pltpu

SSM_GROUP = 16
SSM_STATE = 64
HEAD_DIM = 64
MOBA_BLOCK = 256
MOBA_TOPK = 3
RMS_EPS = 1e-6

LANES = 128
BF16_SUBLANES = 16
VMEM_LIMIT_BYTES = 56 * 1024 * 1024

GROUPS_PER_CLUSTER = LANES // SSM_GROUP
CLUSTER_STATE = GROUPS_PER_CLUSTER * SSM_STATE
TIME_TILE = 32
PERM_T = BF16_SUBLANES
ROW_TILE = 512
FF_CHUNK = 256
MASK_NEG = -1e30

_NT = (((1,), (1,)), ((), ()))


def _rms(x, g):
    return x * lax.rsqrt(jnp.mean(x * x, axis=-1, keepdims=True) + RMS_EPS) * g


def _dot(a, b):
    return jnp.dot(a, b, preferred_element_type=jnp.float32)


def _dot_nt(a, b):
    return lax.dot_general(a, b, _NT, preferred_element_type=jnp.float32)


def _mixer_in_kernel(x_ref, gpre_ref, win_ref, perm_ref, lamre_ref, lamim_ref, bbar_ref, ctil_ref,
                     dskip_ref, wglu_ref, bglu_ref, gssm_ref,
                     q_ref, k_ref, v_ref, yssm_ref,
                     state_ref, utm_ref, bu_ref, s_ref, y_ref, *, n_batch, n_clusters):
    lt = x_ref.shape[1]
    d_model = x_ref.shape[2]
    d_ssm = utm_ref.shape[1]
    rows = n_batch * lt
    perm_rows = n_batch * PERM_T

    @pl.when(pl.program_id(0) == 0)
    def _():
        state_ref[...] = jnp.zeros_like(state_ref)

    x = x_ref[...].reshape(rows, d_model)
    h = _rms(x, gpre_ref[...]).astype(jnp.bfloat16)
    for ref, col in ((q_ref, 1), (k_ref, 2), (v_ref, 3)):
        z = _dot(h, win_ref[:, col * d_ssm:(col + 1) * d_ssm])
        ref[...] = z.astype(jnp.bfloat16).reshape(n_batch, lt, d_ssm)
    u = _dot(h, win_ref[:, 0:d_ssm]).astype(jnp.bfloat16)

    perm = perm_ref[...]
    for th in range(lt // PERM_T):
        chunk = jnp.concatenate(
            [u[b * lt + th * PERM_T: b * lt + (th + 1) * PERM_T] for b in range(n_batch)], axis=0)
        utm_ref[th * perm_rows:(th + 1) * perm_rows, :] = _dot(perm, chunk).astype(jnp.bfloat16)

    for c in range(n_clusters):
        lo, hi = c * CLUSTER_STATE, (c + 1) * CLUSTER_STATE
        bu_ref[...] = _dot(utm_ref[:, c * LANES:(c + 1) * LANES], bbar_ref[c])
        lam_re = lamre_ref[:, lo:hi]
        lam_im = lamim_ref[:, lo:hi]
        h_re = state_ref[0, :, lo:hi]
        h_im = state_ref[1, :, lo:hi]
        for t in range(lt):
            r0 = t * n_batch
            b_re = bu_ref[r0:r0 + n_batch, 0:CLUSTER_STATE]
            b_im = bu_ref[r0:r0 + n_batch, CLUSTER_STATE:2 * CLUSTER_STATE]
            n_re = lam_re * h_re - lam_im * h_im + b_re
            n_im = lam_re * h_im + lam_im * h_re + b_im
            h_re, h_im = n_re, n_im
            s_ref[r0:r0 + n_batch, 0:CLUSTER_STATE] = n_re.astype(jnp.bfloat16)
            s_ref[r0:r0 + n_batch, CLUSTER_STATE:2 * CLUSTER_STATE] = n_im.astype(jnp.bfloat16)
        state_ref[0, :, lo:hi] = h_re
        state_ref[1, :, lo:hi] = h_im
        y_ref[:, c * LANES:(c + 1) * LANES] = _dot(s_ref[...], ctil_ref[c])

    y = y_ref[...] + dskip_ref[...] * utm_ref[...].astype(jnp.float32)
    y = jax.nn.gelu(y)
    y = y * jax.nn.sigmoid(_dot(y.astype(jnp.bfloat16), wglu_ref[...]) + bglu_ref[...])
    yn = _rms(y, gssm_ref[...]).astype(jnp.bfloat16)
    for th in range(lt // PERM_T):
        back = _dot(perm, yn[th * perm_rows:(th + 1) * perm_rows]).astype(jnp.bfloat16)
        for b in range(n_batch):
            yssm_ref[b, th * PERM_T:(th + 1) * PERM_T, :] = back[b * PERM_T:(b + 1) * PERM_T]


def _mixer_in(x, g_pre, w_in, perm, lam_re, lam_im, bbar, ctil, d_skip, w_glu, b_glu, g_ssm):
    n_batch, seq, d_model = x.shape
    d_ssm = w_glu.shape[0]
    n_clusters = d_ssm // LANES
    lt = TIME_TILE
    rows = n_batch * lt
    assert seq % lt == 0 and lt % PERM_T == 0 and n_batch % BF16_SUBLANES == 0
    resident = lambda a: pl.BlockSpec(a.shape, lambda i: (0,) * a.ndim, pipeline_mode=pl.Buffered(1))
    act_spec = pl.BlockSpec((n_batch, lt, d_ssm), lambda i: (0, i, 0))
    act_shape = jax.ShapeDtypeStruct((n_batch, seq, d_ssm), jnp.bfloat16)
    consts = (g_pre, w_in, perm, lam_re, lam_im, bbar, ctil, d_skip, w_glu, b_glu, g_ssm)
    return pl.pallas_call(
        functools.partial(_mixer_in_kernel, n_batch=n_batch, n_clusters=n_clusters),
        out_shape=(act_shape,) * 4,
        grid=(seq // lt,),
        in_specs=[pl.BlockSpec((n_batch, lt, d_model), lambda i: (0, i, 0))]
        + [resident(a) for a in consts],
        out_specs=(act_spec,) * 4,
        scratch_shapes=[
            pltpu.VMEM((2, n_batch, n_clusters * CLUSTER_STATE), jnp.float32),
            pltpu.VMEM((rows, d_ssm), jnp.bfloat16),
            pltpu.VMEM((rows, 2 * CLUSTER_STATE), jnp.float32),
            pltpu.VMEM((rows, 2 * CLUSTER_STATE), jnp.bfloat16),
            pltpu.VMEM((rows, d_ssm), jnp.float32),
        ],
        compiler_params=pltpu.CompilerParams(
            dimension_semantics=("arbitrary",), vmem_limit_bytes=VMEM_LIMIT_BYTES),
        name="mixer_in",
    )(x, *consts)


def _attend(q_aug, ka_ref, v_ref, i, causal):
    r0 = i * MOBA_BLOCK
    s = _dot_nt(q_aug, ka_ref[r0:r0 + MOBA_BLOCK, :])
    s = jnp.where(causal, s, MASK_NEG)
    m = jnp.max(s, axis=-1, keepdims=True)
    p = jnp.exp(s - m)
    l = jnp.sum(p, axis=-1, keepdims=True)
    acc = _dot(p.astype(jnp.bfloat16), v_ref[r0:r0 + MOBA_BLOCK, :])

    def past_block(j, carry):
        m, l, acc = carry
        c0 = pl.multiple_of(j * MOBA_BLOCK, MOBA_BLOCK)
        s = _dot_nt(q_aug, ka_ref[pl.ds(c0, MOBA_BLOCK), :])
        m_new = jnp.maximum(m, jnp.max(s, axis=-1, keepdims=True))
        alpha = jnp.exp(m - m_new)
        p = jnp.exp(s - m_new)
        l = alpha * l + jnp.sum(p, axis=-1, keepdims=True)
        acc = alpha * acc + _dot(p.astype(jnp.bfloat16), v_ref[pl.ds(c0, MOBA_BLOCK), :])
        return m_new, l, acc

    if i > 0:
        m, l, acc = lax.fori_loop(0, i, past_block, (m, l, acc))
    return acc / l


def _moba_kernel(q_ref, k_ref, v_ref, o_ref, ka_ref, *, n_blocks):
    seq = q_ref.shape[0]
    heads_per_slab = LANES // HEAD_DIM
    pair_cols = n_blocks * n_blocks
    lane = lax.broadcasted_iota(jnp.int32, (1, LANES), 1)
    causal = (lax.broadcasted_iota(jnp.int32, (MOBA_BLOCK, MOBA_BLOCK), 1)
              <= lax.broadcasted_iota(jnp.int32, (MOBA_BLOCK, MOBA_BLOCK), 0))

    k_all = k_ref[...].astype(jnp.float32)
    k_mean = jnp.sum(k_all.reshape(n_blocks, MOBA_BLOCK, LANES), axis=1) * (1.0 / MOBA_BLOCK)
    km_a = jnp.broadcast_to(k_mean[:, None, :], (n_blocks, n_blocks, LANES)).reshape(pair_cols, LANES)
    km_b = jnp.broadcast_to(k_mean[None, :, :], (n_blocks, n_blocks, LANES)).reshape(pair_cols, LANES)
    pair = lax.broadcasted_iota(jnp.int32, (1, pair_cols), 1)
    pair_j, pair_jp = pair // n_blocks, pair % n_blocks
    key_blk = lax.broadcasted_iota(jnp.int32, (seq, LANES), 0) // MOBA_BLOCK
    lane_full = lax.broadcasted_iota(jnp.int32, (seq, LANES), 1)

    heads = []
    for hd in range(heads_per_slab):
        own = (lane >= hd * HEAD_DIM) & (lane < (hd + 1) * HEAD_DIM)
        aux = ((hd + 1) % heads_per_slab) * HEAD_DIM
        onehot = jnp.where(lane_full == key_blk + aux, 1.0, 0.0).astype(jnp.bfloat16)
        ka_ref[hd] = jnp.where(own, k_ref[...], onehot)

        def split(m, own=own):
            m = jnp.where(own, m, 0.0)
            hi = m.astype(jnp.bfloat16)
            return hi, (m - hi.astype(jnp.float32)).astype(jnp.bfloat16)
        rank_sum = jnp.where(
            lax.broadcasted_iota(jnp.int32, (pair_cols, LANES), 1)
            == lax.broadcasted_iota(jnp.int32, (pair_cols, LANES), 0) // n_blocks + aux,
            1.0, 0.0).astype(jnp.bfloat16)
        heads.append((own, aux, split(km_a), split(km_b), rank_sum))

    for i in range(n_blocks):
        r0 = i * MOBA_BLOCK
        q_i = q_ref[r0:r0 + MOBA_BLOCK, :] * jnp.bfloat16(HEAD_DIM ** -0.5)
        res = None
        for hd, (own, aux, (kma_hi, kma_lo), (kmb_hi, kmb_lo), rank_sum) in enumerate(heads):
            gate_j = _dot_nt(q_i, kma_hi) + _dot_nt(q_i, kma_lo)
            gate_jp = _dot_nt(q_i, kmb_hi) + _dot_nt(q_i, kmb_lo)
            beats = ((gate_jp > gate_j) | ((gate_jp == gate_j) & (pair_jp < pair_j))) & (pair_jp < i)
            rank = _dot(jnp.where(beats, 1.0, 0.0).astype(jnp.bfloat16), rank_sum)
            blk = lane - aux
            keep = ((rank < MOBA_TOPK) & (blk < i)) | (blk == i) | (blk < 0) | (blk >= n_blocks)
            bias = jnp.where(keep, 0.0, MASK_NEG).astype(jnp.bfloat16)
            q_aug = jnp.where(own, q_i, bias)
            out = _attend(q_aug, ka_ref.at[hd], v_ref, i, causal)
            res = out if res is None else jnp.where(own, out, res)
        o_ref[r0:r0 + MOBA_BLOCK, :] = res.astype(o_ref.dtype)


def _moba(q, k, v):
    n_batch, seq, d_attn = q.shape
    assert seq % MOBA_BLOCK == 0 and d_attn % LANES == 0
    n_blocks = seq // MOBA_BLOCK
    spec = pl.BlockSpec((None, seq, LANES), lambda b, s: (b, 0, s))
    return pl.pallas_call(
        functools.partial(_moba_kernel, n_blocks=n_blocks),
        out_shape=jax.ShapeDtypeStruct((n_batch, seq, d_attn), jnp.bfloat16),
        grid=(n_batch, d_attn // LANES),
        in_specs=[spec, spec, spec],
        out_specs=spec,
        scratch_shapes=[pltpu.VMEM((LANES // HEAD_DIM, seq, LANES), jnp.bfloat16)],
        compiler_params=pltpu.CompilerParams(
            dimension_semantics=("arbitrary", "arbitrary"), vmem_limit_bytes=VMEM_LIMIT_BYTES),
        name="moba",
    )(q, k, v)


def _out_ffn_kernel(x_ref, ys_ref, ya_ref, gattn_ref, wos_ref, woa_ref, gpost_ref, gpre_ref,
                    wg_ref, wu_ref, wd_ref, gffn_ref, o_ref, acc_ref):
    ya = _rms(ya_ref[...].astype(jnp.float32), gattn_ref[...]).astype(jnp.bfloat16)
    mixed = _dot(ys_ref[...], wos_ref[...]) + _dot(ya, woa_ref[...])
    x1 = x_ref[...] + _rms(mixed, gpost_ref[...])
    h = _rms(x1, gpre_ref[...]).astype(jnp.bfloat16)
    d_ff = wg_ref.shape[1]
    for c in range(d_ff // FF_CHUNK):
        lo, hi = c * FF_CHUNK, (c + 1) * FF_CHUNK
        gate = _dot(h, wg_ref[:, lo:hi])
        up = _dot(h, wu_ref[:, lo:hi])
        f = (jax.nn.silu(gate) * up).astype(jnp.bfloat16)
        part = _dot(f, wd_ref[lo:hi, :])
        if c == 0:
            acc_ref[...] = part
        else:
            acc_ref[...] += part
    o_ref[...] = x1 + _rms(acc_ref[...], gffn_ref[...])


def _out_ffn(x2, ys2, ya2, g_attn, wo_s, wo_a, g_post, g_pre, wg, wu, wd, g_ffn):
    n_rows, d_model = x2.shape
    d_half = ys2.shape[1]
    d_ff = wg.shape[1]
    assert n_rows % ROW_TILE == 0 and d_ff % FF_CHUNK == 0
    row = lambda i: (i, 0)
    resident = lambda a: pl.BlockSpec(a.shape, lambda i: (0,) * a.ndim, pipeline_mode=pl.Buffered(1))
    consts = (g_attn, wo_s, wo_a, g_post, g_pre, wg, wu, wd, g_ffn)
    return pl.pallas_call(
        _out_ffn_kernel,
        out_shape=jax.ShapeDtypeStruct((n_rows, d_model), jnp.float32),
        grid=(n_rows // ROW_TILE,),
        in_specs=[
            pl.BlockSpec((ROW_TILE, d_model), row),
            pl.BlockSpec((ROW_TILE, d_half), row),
            pl.BlockSpec((ROW_TILE, d_half), row),
        ] + [resident(a) for a in consts],
        out_specs=pl.BlockSpec((ROW_TILE, d_model), row),
        scratch_shapes=[pltpu.VMEM((ROW_TILE, d_model), jnp.float32)],
        compiler_params=pltpu.CompilerParams(
            dimension_semantics=("arbitrary",), vmem_limit_bytes=VMEM_LIMIT_BYTES),
        name="out_ffn",
    )(x2, ys2, ya2, *consts)


def _s5_params(a_re, a_im, log_dt, b_re, b_im, c_re, c_im, n_batch):
    f32 = jnp.float32
    a_re, a_im = a_re.astype(f32), a_im.astype(f32)
    b_re, b_im = b_re.astype(f32), b_im.astype(f32)
    n_groups = a_re.shape[0]
    n_clusters = n_groups // GROUPS_PER_CLUSTER
    dt = jnp.exp(log_dt.astype(f32))[:, None]
    mag = jnp.exp(a_re * dt)
    ang = a_im * dt
    lb_re = mag * jnp.cos(ang)
    lb_im = mag * jnp.sin(ang)
    nr = lb_re - 1.0
    den = a_re * a_re + a_im * a_im
    cr = (nr * a_re + lb_im * a_im) / den
    ci = (lb_im * a_re - nr * a_im) / den
    bb_re = cr[..., None] * b_re - ci[..., None] * b_im
    bb_im = cr[..., None] * b_im + ci[..., None] * b_re
    eye = jnp.eye(GROUPS_PER_CLUSTER, dtype=f32)

    def in_proj(bb):
        bb = bb.reshape(n_clusters, GROUPS_PER_CLUSTER, SSM_STATE, SSM_GROUP)
        m = jnp.einsum('cgph,gk->cghkp', bb, eye)
        return m.reshape(n_clusters, LANES, CLUSTER_STATE)

    def out_proj(cc):
        cc = cc.reshape(n_clusters, GROUPS_PER_CLUSTER, SSM_GROUP, SSM_STATE)
        m = jnp.einsum('cghp,gk->cgpkh', cc, eye)
        return m.reshape(n_clusters, CLUSTER_STATE, LANES)

    bbar = jnp.concatenate([in_proj(bb_re), in_proj(bb_im)], axis=2).astype(jnp.bfloat16)
    ctil = jnp.concatenate([out_proj(c_re.astype(f32)), -out_proj(c_im.astype(f32))],
                           axis=1).astype(jnp.bfloat16)
    lam_re = jnp.broadcast_to(lb_re.reshape(1, -1), (n_batch, lb_re.size))
    lam_im = jnp.broadcast_to(lb_im.reshape(1, -1), (n_batch, lb_im.size))
    return lam_re, lam_im, bbar, ctil


def _time_batch_permutation(n_batch):
    n = n_batch * PERM_T
    r = jnp.arange(n)
    src = (r % n_batch) * PERM_T + r // n_batch
    return (src[:, None] == jnp.arange(n)[None, :]).astype(jnp.bfloat16)


def _layer(x, g_pre_mix, w_in, ssm_a_re, ssm_a_im, ssm_log_dt, ssm_b_re, ssm_b_im, ssm_c_re,
           ssm_c_im, ssm_d, w_glu, b_glu, g_ssm_out, g_attn_out, w_out, g_post_mix, g_pre_ffn,
           w_gate, w_up, w_down, g_post_ffn):
    n_batch, seq, d_model = x.shape
    d_ssm = w_glu.shape[0]
    bf16 = jnp.bfloat16
    row = lambda g: g.reshape(1, -1).astype(jnp.float32)
    lam_re, lam_im, bbar, ctil = _s5_params(ssm_a_re, ssm_a_im, ssm_log_dt, ssm_b_re, ssm_b_im,
                                            ssm_c_re, ssm_c_im, n_batch)
    q, k, v, y_ssm = _mixer_in(
        x, row(g_pre_mix), w_in.astype(bf16), _time_batch_permutation(n_batch), lam_re, lam_im,
        bbar, ctil, row(ssm_d), w_glu.astype(bf16), row(b_glu), row(g_ssm_out))
    y_attn = _moba(q, k, v)
    n_rows = n_batch * seq
    out = _out_ffn(
        x.reshape(n_rows, d_model), y_ssm.reshape(n_rows, d_ssm), y_attn.reshape(n_rows, -1),
        row(g_attn_out), w_out[:d_ssm].astype(bf16), w_out[d_ssm:].astype(bf16), row(g_post_mix),
        row(g_pre_ffn), w_gate.astype(bf16), w_up.astype(bf16), w_down.astype(bf16),
        row(g_post_ffn))
    return out.reshape(n_batch, seq, d_model)


def kernel(x, g_pre_mix, w_in, ssm_a_re, ssm_a_im, ssm_log_dt, ssm_b_re, ssm_b_im, ssm_c_re, ssm_c_im, ssm_d, w_glu, b_glu, g_ssm_out, g_attn_out, w_out, g_post_mix, g_pre_ffn, w_gate, w_up, w_down, g_post_ffn):
    for l in range(w_in.shape[0]):
        x = _layer(x, g_pre_mix[l], w_in[l], ssm_a_re[l], ssm_a_im[l], ssm_log_dt[l], ssm_b_re[l],
                   ssm_b_im[l], ssm_c_re[l], ssm_c_im[l], ssm_d[l], w_glu[l], b_glu[l],
                   g_ssm_out[l], g_attn_out[l], w_out[l], g_post_mix[l], g_pre_ffn[l], w_gate[l],
                   w_up[l], w_down[l], g_post_ffn[l])
    return x
```

```python
import functools

import jax
import jax.numpy as jnp
from jax import lax
from jax.experimental import pallas as pl
from jax.experimental.pallas import tpu as pltpu

SSM_GROUP = 16
SSM_STATE = 64
HEAD_DIM = 64
MOBA_BLOCK = 256
MOBA_TOPK = 3
RMS_EPS = 1e-6

LANES = 128
BF16_SUBLANES = 16
VMEM_LIMIT_BYTES = 56 * 1024 * 1024

GROUPS_PER_CLUSTER = LANES // SSM_GROUP
CLUSTER_STATE = GROUPS_PER_CLUSTER * SSM_STATE
TIME_TILE = 32
PERM_T = BF16_SUBLANES
ROW_TILE = 512
FF_CHUNK = 256
MASK_NEG = -1e30

_NT = (((1,), (1,)), ((), ()))


def _rms(x, g):
    return x * lax.rsqrt(jnp.mean(x * x, axis=-1, keepdims=True) + RMS_EPS) * g


def _dot(a, b):
    return jnp.dot(a, b, preferred_element_type=jnp.float32)


def _dot_nt(a, b):
    return lax.dot_general(a, b, _NT, preferred_element_type=jnp.float32)


def _mixer_in_kernel(x_ref, gpre_ref, win_ref, perm_ref, lamre_ref, lamim_ref, bbar_ref, ctil_ref,
                     dskip_ref, wglu_ref, bglu_ref, gssm_ref,
                     q_ref, k_ref, v_ref, yssm_ref,
                     state_ref, utm_ref, bu_ref, s_ref, y_ref, *, n_batch, n_clusters):
    lt = x_ref.shape[1]
    d_model = x_ref.shape[2]
    d_ssm = utm_ref.shape[1]
    rows = n_batch * lt
    perm_rows = n_batch * PERM_T

    @pl.when(pl.program_id(0) == 0)
    def _():
        state_ref[...] = jnp.zeros_like(state_ref)

    x = x_ref[...].reshape(rows, d_model)
    h = _rms(x, gpre_ref[...]).astype(jnp.bfloat16)
    for ref, col in ((q_ref, 1), (k_ref, 2), (v_ref, 3)):
        z = _dot(h, win_ref[:, col * d_ssm:(col + 1) * d_ssm])
        ref[...] = z.astype(jnp.bfloat16).reshape(n_batch, lt, d_ssm)
    u = _dot(h, win_ref[:, 0:d_ssm]).astype(jnp.bfloat16)

    perm = perm_ref[...]
    for th in range(lt // PERM_T):
        chunk = jnp.concatenate(
            [u[b * lt + th * PERM_T: b * lt + (th + 1) * PERM_T] for b in range(n_batch)], axis=0)
        utm_ref[th * perm_rows:(th + 1) * perm_rows, :] = _dot(perm, chunk).astype(jnp.bfloat16)

    for c in range(n_clusters):
        lo, hi = c * CLUSTER_STATE, (c + 1) * CLUSTER_STATE
        bu_ref[...] = _dot(utm_ref[:, c * LANES:(c + 1) * LANES], bbar_ref[c])
        lam_re = lamre_ref[:, lo:hi]
        lam_im = lamim_ref[:, lo:hi]
        h_re = state_ref[0, :, lo:hi]
        h_im = state_ref[1, :, lo:hi]
        for t in range(lt):
            r0 = t * n_batch
            b_re = bu_ref[r0:r0 + n_batch, 0:CLUSTER_STATE]
            b_im = bu_ref[r0:r0 + n_batch, CLUSTER_STATE:2 * CLUSTER_STATE]
            n_re = lam_re * h_re - lam_im * h_im + b_re
            n_im = lam_re * h_im + lam_im * h_re + b_im
            h_re, h_im = n_re, n_im
            s_ref[r0:r0 + n_batch, 0:CLUSTER_STATE] = n_re.astype(jnp.bfloat16)
            s_ref[r0:r0 + n_batch, CLUSTER_STATE:2 * CLUSTER_STATE] = n_im.astype(jnp.bfloat16)
        state_ref[0, :, lo:hi] = h_re
        state_ref[1, :, lo:hi] = h_im
        y_ref[:, c * LANES:(c + 1) * LANES] = _dot(s_ref[...], ctil_ref[c])

    y = y_ref[...] + dskip_ref[...] * utm_ref[...].astype(jnp.float32)
    y = jax.nn.gelu(y)
    y = y * jax.nn.sigmoid(_dot(y.astype(jnp.bfloat16), wglu_ref[...]) + bglu_ref[...])
    yn = _rms(y, gssm_ref[...]).astype(jnp.bfloat16)
    for th in range(lt // PERM_T):
        back = _dot(perm, yn[th * perm_rows:(th + 1) * perm_rows]).astype(jnp.bfloat16)
        for b in range(n_batch):
            yssm_ref[b, th * PERM_T:(th + 1) * PERM_T, :] = back[b * PERM_T:(b + 1) * PERM_T]


def _mixer_in(x, g_pre, w_in, perm, lam_re, lam_im, bbar, ctil, d_skip, w_glu, b_glu, g_ssm):
    n_batch, seq, d_model = x.shape
    d_ssm = w_glu.shape[0]
    n_clusters = d_ssm // LANES
    lt = TIME_TILE
    rows = n_batch * lt
    assert seq % lt == 0 and lt % PERM_T == 0 and n_batch % BF16_SUBLANES == 0
    resident = lambda a: pl.BlockSpec(a.shape, lambda i: (0,) * a.ndim, pipeline_mode=pl.Buffered(1))
    act_spec = pl.BlockSpec((n_batch, lt, d_ssm), lambda i: (0, i, 0))
    act_shape = jax.ShapeDtypeStruct((n_batch, seq, d_ssm), jnp.bfloat16)
    consts = (g_pre, w_in, perm, lam_re, lam_im, bbar, ctil, d_skip, w_glu, b_glu, g_ssm)
    return pl.pallas_call(
        functools.partial(_mixer_in_kernel, n_batch=n_batch, n_clusters=n_clusters),
        out_shape=(act_shape,) * 4,
        grid=(seq // lt,),
        in_specs=[pl.BlockSpec((n_batch, lt, d_model), lambda i: (0, i, 0))]
        + [resident(a) for a in consts],
        out_specs=(act_spec,) * 4,
        scratch_shapes=[
            pltpu.VMEM((2, n_batch, n_clusters * CLUSTER_STATE), jnp.float32),
            pltpu.VMEM((rows, d_ssm), jnp.bfloat16),
            pltpu.VMEM((rows, 2 * CLUSTER_STATE), jnp.float32),
            pltpu.VMEM((rows, 2 * CLUSTER_STATE), jnp.bfloat16),
            pltpu.VMEM((rows, d_ssm), jnp.float32),
        ],
        compiler_params=pltpu.CompilerParams(
            dimension_semantics=("arbitrary",), vmem_limit_bytes=VMEM_LIMIT_BYTES),
        name="mixer_in",
    )(x, *consts)


def _attend(q_aug, ka_ref, v_ref, i, causal):
    r0 = i * MOBA_BLOCK
    s_own = jnp.where(causal, _dot_nt(q_aug, ka_ref[r0:r0 + MOBA_BLOCK, :]), MASK_NEG)
    m = jnp.max(s_own, axis=-1, keepdims=True)
    if i > 0:
        s_past = _dot_nt(q_aug, ka_ref[0:r0, :])
        m = jnp.maximum(m, jnp.max(s_past, axis=-1, keepdims=True))
    p = jnp.exp(s_own - m)
    l = jnp.sum(p, axis=-1, keepdims=True)
    acc = _dot(p.astype(jnp.bfloat16), v_ref[r0:r0 + MOBA_BLOCK, :])
    if i > 0:
        p = jnp.exp(s_past - m)
        l = l + jnp.sum(p, axis=-1, keepdims=True)
        acc = acc + _dot(p.astype(jnp.bfloat16), v_ref[0:r0, :])
    return acc / l


def _moba_kernel(q_ref, k_ref, v_ref, o_ref, qa_ref, ka_ref, *, n_blocks):
    seq = q_ref.shape[0]
    heads_per_slab = LANES // HEAD_DIM
    pair_cols = n_blocks * n_blocks
    lane = lax.broadcasted_iota(jnp.int32, (1, LANES), 1)
    causal = (lax.broadcasted_iota(jnp.int32, (MOBA_BLOCK, MOBA_BLOCK), 1)
              <= lax.broadcasted_iota(jnp.int32, (MOBA_BLOCK, MOBA_BLOCK), 0))

    k_all = k_ref[...].astype(jnp.float32)
    k_mean = jnp.sum(k_all.reshape(n_blocks, MOBA_BLOCK, LANES), axis=1) * (1.0 / MOBA_BLOCK)
    km_a = jnp.broadcast_to(k_mean[:, None, :], (n_blocks, n_blocks, LANES)).reshape(pair_cols, LANES)
    km_b = jnp.broadcast_to(k_mean[None, :, :], (n_blocks, n_blocks, LANES)).reshape(pair_cols, LANES)
    pair = lax.broadcasted_iota(jnp.int32, (1, pair_cols), 1)
    pair_j, pair_jp = pair // n_blocks, pair % n_blocks
    row_blk = lax.broadcasted_iota(jnp.int32, (seq, 1), 0) // MOBA_BLOCK
    lane_full = lax.broadcasted_iota(jnp.int32, (seq, LANES), 1)
    q_all = q_ref[...] * jnp.bfloat16(HEAD_DIM ** -0.5)

    owns = []
    for hd in range(heads_per_slab):
        own = (lane >= hd * HEAD_DIM) & (lane < (hd + 1) * HEAD_DIM)
        aux = ((hd + 1) % heads_per_slab) * HEAD_DIM
        owns.append(own)
        onehot = jnp.where(lane_full == row_blk + aux, 1.0, 0.0).astype(jnp.bfloat16)
        ka_ref[hd] = jnp.where(own, k_ref[...], onehot)

        def split(m, own=own):
            m = jnp.where(own, m, 0.0)
            hi = m.astype(jnp.bfloat16)
            return hi, (m - hi.astype(jnp.float32)).astype(jnp.bfloat16)
        kma_hi, kma_lo = split(km_a)
        kmb_hi, kmb_lo = split(km_b)
        rank_sum = jnp.where(
            lax.broadcasted_iota(jnp.int32, (pair_cols, LANES), 1)
            == lax.broadcasted_iota(jnp.int32, (pair_cols, LANES), 0) // n_blocks + aux,
            1.0, 0.0).astype(jnp.bfloat16)
        gate_j = _dot_nt(q_all, kma_hi) + _dot_nt(q_all, kma_lo)
        gate_jp = _dot_nt(q_all, kmb_hi) + _dot_nt(q_all, kmb_lo)
        beats = (((gate_jp > gate_j) | ((gate_jp == gate_j) & (pair_jp < pair_j)))
                 & (pair_jp < row_blk))
        rank = _dot(jnp.where(beats, 1.0, 0.0).astype(jnp.bfloat16), rank_sum)
        blk = lane - aux
        keep = (((rank < MOBA_TOPK) & (blk < row_blk)) | (blk == row_blk)
                | (blk < 0) | (blk >= n_blocks))
        qa_ref[hd] = jnp.where(own, q_all, jnp.where(keep, 0.0, MASK_NEG).astype(jnp.bfloat16))

    for i in range(n_blocks):
        r0 = i * MOBA_BLOCK
        res = None
        for hd, own in enumerate(owns):
            out = _attend(qa_ref[hd, r0:r0 + MOBA_BLOCK, :], ka_ref.at[hd], v_ref, i, causal)
            res = out if res is None else jnp.where(own, out, res)
        o_ref[r0:r0 + MOBA_BLOCK, :] = res.astype(o_ref.dtype)


def _moba(q, k, v):
    n_batch, seq, d_attn = q.shape
    assert seq % MOBA_BLOCK == 0 and d_attn % LANES == 0
    n_blocks = seq // MOBA_BLOCK
    spec = pl.BlockSpec((None, seq, LANES), lambda b, s: (b, 0, s))
    return pl.pallas_call(
        functools.partial(_moba_kernel, n_blocks=n_blocks),
        out_shape=jax.ShapeDtypeStruct((n_batch, seq, d_attn), jnp.bfloat16),
        grid=(n_batch, d_attn // LANES),
        in_specs=[spec, spec, spec],
        out_specs=spec,
        scratch_shapes=[pltpu.VMEM((LANES // HEAD_DIM, seq, LANES), jnp.bfloat16),
                        pltpu.VMEM((LANES // HEAD_DIM, seq, LANES), jnp.bfloat16)],
        compiler_params=pltpu.CompilerParams(
            dimension_semantics=("arbitrary", "arbitrary"), vmem_limit_bytes=VMEM_LIMIT_BYTES),
        name="moba",
    )(q, k, v)


def _out_ffn_kernel(x_ref, ys_ref, ya_ref, gattn_ref, wos_ref, woa_ref, gpost_ref, gpre_ref,
                    wg_ref, wu_ref, wd_ref, gffn_ref, o_ref, acc_ref):
    ya = _rms(ya_ref[...].astype(jnp.float32), gattn_ref[...]).astype(jnp.bfloat16)
    mixed = _dot(ys_ref[...], wos_ref[...]) + _dot(ya, woa_ref[...])
    x1 = x_ref[...] + _rms(mixed, gpost_ref[...])
    h = _rms(x1, gpre_ref[...]).astype(jnp.bfloat16)
    d_ff = wg_ref.shape[1]
    for c in range(d_ff // FF_CHUNK):
        lo, hi = c * FF_CHUNK, (c + 1) * FF_CHUNK
        gate = _dot(h, wg_ref[:, lo:hi])
        up = _dot(h, wu_ref[:, lo:hi])
        f = (jax.nn.silu(gate) * up).astype(jnp.bfloat16)
        part = _dot(f, wd_ref[lo:hi, :])
        if c == 0:
            acc_ref[...] = part
        else:
            acc_ref[...] += part
    o_ref[...] = x1 + _rms(acc_ref[...], gffn_ref[...])


def _out_ffn(x2, ys2, ya2, g_attn, wo_s, wo_a, g_post, g_pre, wg, wu, wd, g_ffn):
    n_rows, d_model = x2.shape
    d_half = ys2.shape[1]
    d_ff = wg.shape[1]
    assert n_rows % ROW_TILE == 0 and d_ff % FF_CHUNK == 0
    row = lambda i: (i, 0)
    resident = lambda a: pl.BlockSpec(a.shape, lambda i: (0,) * a.ndim, pipeline_mode=pl.Buffered(1))
    consts = (g_attn, wo_s, wo_a, g_post, g_pre, wg, wu, wd, g_ffn)
    return pl.pallas_call(
        _out_ffn_kernel,
        out_shape=jax.ShapeDtypeStruct((n_rows, d_model), jnp.float32),
        grid=(n_rows // ROW_TILE,),
        in_specs=[
            pl.BlockSpec((ROW_TILE, d_model), row),
            pl.BlockSpec((ROW_TILE, d_half), row),
            pl.BlockSpec((ROW_TILE, d_half), row),
        ] + [resident(a) for a in consts],
        out_specs=pl.BlockSpec((ROW_TILE, d_model), row),
        scratch_shapes=[pltpu.VMEM((ROW_TILE, d_model), jnp.float32)],
        compiler_params=pltpu.CompilerParams(
            dimension_semantics=("arbitrary",), vmem_limit_bytes=VMEM_LIMIT_BYTES),
        name="out_ffn",
    )(x2, ys2, ya2, *consts)


def _s5_params(a_re, a_im, log_dt, b_re, b_im, c_re, c_im, n_batch):
    f32 = jnp.float32
    a_re, a_im = a_re.astype(f32), a_im.astype(f32)
    b_re, b_im = b_re.astype(f32), b_im.astype(f32)
    n_groups = a_re.shape[0]
    n_clusters = n_groups // GROUPS_PER_CLUSTER
    dt = jnp.exp(log_dt.astype(f32))[:, None]
    mag = jnp.exp(a_re * dt)
    ang = a_im * dt
    lb_re = mag * jnp.cos(ang)
    lb_im = mag * jnp.sin(ang)
    nr = lb_re - 1.0
    den = a_re * a_re + a_im * a_im
    cr = (nr * a_re + lb_im * a_im) / den
    ci = (lb_im * a_re - nr * a_im) / den
    bb_re = cr[..., None] * b_re - ci[..., None] * b_im
    bb_im = cr[..., None] * b_im + ci[..., None] * b_re
    eye = jnp.eye(GROUPS_PER_CLUSTER, dtype=f32)

    def in_proj(bb):
        bb = bb.reshape(n_clusters, GROUPS_PER_CLUSTER, SSM_STATE, SSM_GROUP)
        m = jnp.einsum('cgph,gk->cghkp', bb, eye)
        return m.reshape(n_clusters, LANES, CLUSTER_STATE)

    def out_proj(cc):
        cc = cc.reshape(n_clusters, GROUPS_PER_CLUSTER, SSM_GROUP, SSM_STATE)
        m = jnp.einsum('cghp,gk->cgpkh', cc, eye)
        return m.reshape(n_clusters, CLUSTER_STATE, LANES)

    bbar = jnp.concatenate([in_proj(bb_re), in_proj(bb_im)], axis=2).astype(jnp.bfloat16)
    ctil = jnp.concatenate([out_proj(c_re.astype(f32)), -out_proj(c_im.astype(f32))],
                           axis=1).astype(jnp.bfloat16)
    lam_re = jnp.broadcast_to(lb_re.reshape(1, -1), (n_batch, lb_re.size))
    lam_im = jnp.broadcast_to(lb_im.reshape(1, -1), (n_batch, lb_im.size))
    return lam_re, lam_im, bbar, ctil


def _time_batch_permutation(n_batch):
    n = n_batch * PERM_T
    r = jnp.arange(n)
    src = (r % n_batch) * PERM_T + r // n_batch
    return (src[:, None] == jnp.arange(n)[None, :]).astype(jnp.bfloat16)


def _layer(x, g_pre_mix, w_in, ssm_a_re, ssm_a_im, ssm_log_dt, ssm_b_re, ssm_b_im, ssm_c_re,
           ssm_c_im, ssm_d, w_glu, b_glu, g_ssm_out, g_attn_out, w_out, g_post_mix, g_pre_ffn,
           w_gate, w_up, w_down, g_post_ffn):
    n_batch, seq, d_model = x.shape
    d_ssm = w_glu.shape[0]
    bf16 = jnp.bfloat16
    row = lambda g: g.reshape(1, -1).astype(jnp.float32)
    lam_re, lam_im, bbar, ctil = _s5_params(ssm_a_re, ssm_a_im, ssm_log_dt, ssm_b_re, ssm_b_im,
                                            ssm_c_re, ssm_c_im, n_batch)
    q, k, v, y_ssm = _mixer_in(
        x, row(g_pre_mix), w_in.astype(bf16), _time_batch_permutation(n_batch), lam_re, lam_im,
        bbar, ctil, row(ssm_d), w_glu.astype(bf16), row(b_glu), row(g_ssm_out))
    y_attn = _moba(q, k, v)
    n_rows = n_batch * seq
    out = _out_ffn(
        x.reshape(n_rows, d_model), y_ssm.reshape(n_rows, d_ssm), y_attn.reshape(n_rows, -1),
        row(g_attn_out), w_out[:d_ssm].astype(bf16), w_out[d_ssm:].astype(bf16), row(g_post_mix),
        row(g_pre_ffn), w_gate.astype(bf16), w_up.astype(bf16), w_down.astype(bf16),
        row(g_post_ffn))
    return out.reshape(n_batch, seq, d_model)


def kernel(x, g_pre_mix, w_in, ssm_a_re, ssm_a_im, ssm_log_dt, ssm_b_re, ssm_b_im, ssm_c_re, ssm_c_im, ssm_d, w_glu, b_glu, g_ssm_out, g_attn_out, w_out, g_post_mix, g_pre_ffn, w_gate, w_up, w_down, g_post_ffn):
    for l in range(w_in.shape[0]):
        x = _layer(x, g_pre_mix[l], w_in[l], ssm_a_re[l], ssm_a_im[l], ssm_log_dt[l], ssm_b_re[l],
                   ssm_b_im[l], ssm_c_re[l], ssm_c_im[l], ssm_d[l], w_glu[l], b_glu[l],
                   g_ssm_out[l], g_attn_out[l], w_out[l], g_post_mix[l], g_pre_ffn[l], w_gate[l],
                   w_up[l], w_down[l], g_post_ffn[l])
    return x
```

```python
import functools
import math

import jax
import jax.numpy as jnp
from jax import lax
from jax.experimental import pallas as pl
from jax.experimental.pallas import tpu as pltpu

SSM_GROUP = 16
SSM_STATE = 64
HEAD_DIM = 64
MOBA_BLOCK = 256
MOBA_TOPK = 3
RMS_EPS = 1e-6

LANES = 128
BF16_SUBLANES = 16
VMEM_LIMIT_BYTES = 56 * 1024 * 1024

GROUPS_PER_CLUSTER = LANES // SSM_GROUP
CLUSTER_STATE = GROUPS_PER_CLUSTER * SSM_STATE
TIME_TILE = 32
PERM_T = BF16_SUBLANES
MOBA_SLABS = 2
ROW_TILE = 1024
FF_CHUNK = 256
MASK_NEG = -1e30
LOG2_E = math.log2(math.e)

_NT = (((1,), (1,)), ((), ()))


def _rms(x, g):
    return x * lax.rsqrt(jnp.mean(x * x, axis=-1, keepdims=True) + RMS_EPS) * g


def _dot(a, b):
    return jnp.dot(a, b, preferred_element_type=jnp.float32)


def _dot_nt(a, b):
    return lax.dot_general(a, b, _NT, preferred_element_type=jnp.float32)


def _mixer_in_kernel(x_ref, gpre_ref, win_ref, perm_ref, lamre_ref, lamim_ref, bbar_ref, ctil_ref,
                     dskip_ref, wglu_ref, bglu_ref, gssm_ref,
                     q_ref, k_ref, v_ref, yssm_ref,
                     state_ref, utm_ref, bu_ref, s_ref, y_ref, *, n_batch, n_clusters):
    lt = x_ref.shape[1]
    d_model = x_ref.shape[2]
    d_ssm = utm_ref.shape[1]
    rows = n_batch * lt
    perm_rows = n_batch * PERM_T

    @pl.when(pl.program_id(0) == 0)
    def _():
        state_ref[...] = jnp.zeros_like(state_ref)

    x = x_ref[...].reshape(rows, d_model)
    h = _rms(x, gpre_ref[...]).astype(jnp.bfloat16)
    u = _dot(h, win_ref[:, 0:d_ssm]).astype(jnp.bfloat16)

    proj_cols = 2 * LANES
    projections = [(ref, col, tile) for ref, col in ((q_ref, 1), (k_ref, 2), (v_ref, 3))
                   for tile in range(d_ssm // proj_cols)]

    def project(count):
        for _ in range(min(count, len(projections))):
            ref, col, tile = projections.pop(0)
            c0 = col * d_ssm + tile * proj_cols
            z = _dot(h, win_ref[:, c0:c0 + proj_cols]).astype(jnp.bfloat16)
            ref[:, :, tile * proj_cols:(tile + 1) * proj_cols] = z.reshape(n_batch, lt, proj_cols)

    project(1)
    perm = perm_ref[...]
    for th in range(lt // PERM_T):
        chunk = jnp.concatenate(
            [u[b * lt + th * PERM_T: b * lt + (th + 1) * PERM_T] for b in range(n_batch)], axis=0)
        utm_ref[th * perm_rows:(th + 1) * perm_rows, :] = _dot(perm, chunk).astype(jnp.bfloat16)
    for c in range(n_clusters):
        bu_ref[c] = _dot(utm_ref[:, c * LANES:(c + 1) * LANES], bbar_ref[c])
        project(1)

    for c in range(n_clusters):
        lo, hi = c * CLUSTER_STATE, (c + 1) * CLUSTER_STATE
        lam_re = lamre_ref[:, lo:hi]
        lam_im = lamim_ref[:, lo:hi]
        h_re = state_ref[0, :, lo:hi]
        h_im = state_ref[1, :, lo:hi]
        for t in range(lt):
            r0 = t * n_batch
            b_re = bu_ref[c, r0:r0 + n_batch, 0:CLUSTER_STATE]
            b_im = bu_ref[c, r0:r0 + n_batch, CLUSTER_STATE:2 * CLUSTER_STATE]
            n_re = lam_re * h_re - lam_im * h_im + b_re
            n_im = lam_re * h_im + lam_im * h_re + b_im
            h_re, h_im = n_re, n_im
            s_ref[c, r0:r0 + n_batch, 0:CLUSTER_STATE] = n_re.astype(jnp.bfloat16)
            s_ref[c, r0:r0 + n_batch, CLUSTER_STATE:2 * CLUSTER_STATE] = n_im.astype(jnp.bfloat16)
        state_ref[0, :, lo:hi] = h_re
        state_ref[1, :, lo:hi] = h_im
        y_ref[:, c * LANES:(c + 1) * LANES] = _dot(s_ref[c], ctil_ref[c])

    y = y_ref[...] + dskip_ref[...] * utm_ref[...].astype(jnp.float32)
    y = jax.nn.gelu(y)
    glu = _dot(y.astype(jnp.bfloat16), wglu_ref[...])
    project(len(projections))
    y = y * jax.nn.sigmoid(glu + bglu_ref[...])
    yn = _rms(y, gssm_ref[...]).astype(jnp.bfloat16)
    for th in range(lt // PERM_T):
        back = _dot(perm, yn[th * perm_rows:(th + 1) * perm_rows]).astype(jnp.bfloat16)
        for b in range(n_batch):
            yssm_ref[b, th * PERM_T:(th + 1) * PERM_T, :] = back[b * PERM_T:(b + 1) * PERM_T]


def _mixer_in(x, g_pre, w_in, perm, lam_re, lam_im, bbar, ctil, d_skip, w_glu, b_glu, g_ssm):
    n_batch, seq, d_model = x.shape
    d_ssm = w_glu.shape[0]
    n_clusters = d_ssm // LANES
    lt = TIME_TILE
    rows = n_batch * lt
    assert seq % lt == 0 and lt % PERM_T == 0 and n_batch % BF16_SUBLANES == 0
    resident = lambda a: pl.BlockSpec(a.shape, lambda i: (0,) * a.ndim, pipeline_mode=pl.Buffered(1))
    act_spec = pl.BlockSpec((n_batch, lt, d_ssm), lambda i: (0, i, 0))
    act_shape = jax.ShapeDtypeStruct((n_batch, seq, d_ssm), jnp.bfloat16)
    consts = (g_pre, w_in, perm, lam_re, lam_im, bbar, ctil, d_skip, w_glu, b_glu, g_ssm)
    return pl.pallas_call(
        functools.partial(_mixer_in_kernel, n_batch=n_batch, n_clusters=n_clusters),
        out_shape=(act_shape,) * 4,
        grid=(seq // lt,),
        in_specs=[pl.BlockSpec((n_batch, lt, d_model), lambda i: (0, i, 0))]
        + [resident(a) for a in consts],
        out_specs=(act_spec,) * 4,
        scratch_shapes=[
            pltpu.VMEM((2, n_batch, n_clusters * CLUSTER_STATE), jnp.float32),
            pltpu.VMEM((rows, d_ssm), jnp.bfloat16),
            pltpu.VMEM((n_clusters, rows, 2 * CLUSTER_STATE), jnp.float32),
            pltpu.VMEM((n_clusters, rows, 2 * CLUSTER_STATE), jnp.bfloat16),
            pltpu.VMEM((rows, d_ssm), jnp.float32),
        ],
        compiler_params=pltpu.CompilerParams(
            dimension_semantics=("arbitrary",), vmem_limit_bytes=VMEM_LIMIT_BYTES),
        name="mixer_in",
    )(x, *consts)


def _scores(q_aug, ka_ref, i, causal):
    r0 = i * MOBA_BLOCK
    s_own = jnp.where(causal, _dot_nt(q_aug, ka_ref[r0:r0 + MOBA_BLOCK, :]), MASK_NEG)
    s_past = _dot_nt(q_aug, ka_ref[0:r0, :]) if i > 0 else None
    return s_own, s_past


def _softmax_pv(scores, va_ref, i):
    s_own, s_past = scores
    r0 = i * MOBA_BLOCK
    m = jnp.max(s_own, axis=-1, keepdims=True)
    if s_past is not None:
        m = jnp.maximum(m, jnp.max(s_past, axis=-1, keepdims=True))
    acc = _dot(jnp.exp2(s_own - m).astype(jnp.bfloat16), va_ref[r0:r0 + MOBA_BLOCK, :])
    if s_past is not None:
        acc = acc + _dot(jnp.exp2(s_past - m).astype(jnp.bfloat16), va_ref[0:r0, :])
    return acc[:, :LANES] / acc[:, LANES:]


def _moba_kernel(q_ref, k_ref, v_ref, o_ref, ka_ref, va_ref, *, n_blocks):
    for s in range(q_ref.shape[1] // LANES):
        cols = slice(s * LANES, (s + 1) * LANES)
        _moba_slab(q_ref.at[:, cols], k_ref.at[:, cols], v_ref.at[:, cols], o_ref.at[:, cols],
                   ka_ref.at[s], va_ref.at[s], n_blocks)


def _moba_slab(q_ref, k_ref, v_ref, o_ref, ka_ref, va_ref, n_blocks):
    seq = q_ref.shape[0]
    n_pairs = n_blocks * n_blocks
    assert LANES == 2 * HEAD_DIM and n_pairs <= HEAD_DIM
    lane = lax.broadcasted_iota(jnp.int32, (1, LANES), 1)
    own0 = lane < HEAD_DIM
    causal = (lax.broadcasted_iota(jnp.int32, (MOBA_BLOCK, MOBA_BLOCK), 1)
              <= lax.broadcasted_iota(jnp.int32, (MOBA_BLOCK, MOBA_BLOCK), 0))

    key_blk = lax.broadcasted_iota(jnp.int32, (seq, 1), 0) // MOBA_BLOCK
    lane_full = lax.broadcasted_iota(jnp.int32, (seq, LANES), 1)
    k_all = k_ref[...]
    ka_ref[0] = jnp.where(own0, k_all,
                          jnp.where(lane_full == key_blk + HEAD_DIM, 1.0, 0.0).astype(jnp.bfloat16))
    ka_ref[1] = jnp.where(own0, jnp.where(lane_full == key_blk, 1.0, 0.0).astype(jnp.bfloat16), k_all)
    va_ref[:, :LANES] = v_ref[...]
    va_ref[:, LANES:] = jnp.ones((seq, LANES), jnp.bfloat16)

    k_mean = jnp.sum(k_all.astype(jnp.float32).reshape(n_blocks, MOBA_BLOCK, LANES), axis=1)
    k_mean = k_mean * (1.0 / MOBA_BLOCK)
    km_a = jnp.broadcast_to(k_mean[:, None, :], (n_blocks, n_blocks, LANES)).reshape(n_pairs, LANES)
    km_b = jnp.broadcast_to(k_mean[None, :, :], (n_blocks, n_blocks, LANES)).reshape(n_pairs, LANES)
    pad = [jnp.zeros((HEAD_DIM - n_pairs, LANES), jnp.float32)] if n_pairs < HEAD_DIM else []

    def both_heads(km):
        return jnp.concatenate(
            [jnp.where(own0, km, 0.0)] + pad + [jnp.where(own0, 0.0, km)] + pad, axis=0)

    def split(m):
        hi = m.astype(jnp.bfloat16)
        return [hi, (m - hi.astype(jnp.float32)).astype(jnp.bfloat16)]
    gate_rhs = jnp.concatenate(split(both_heads(km_a)) + split(both_heads(km_b)), axis=0)

    trip = lane % HEAD_DIM
    trip_j, trip_jp = trip // n_blocks, trip % n_blocks
    trip_ok = trip < n_pairs
    r_idx = lax.broadcasted_iota(jnp.int32, (LANES, LANES), 0)
    c_idx = lax.broadcasted_iota(jnp.int32, (LANES, LANES), 1)
    r_bias_lane = jnp.where(r_idx < HEAD_DIM, HEAD_DIM, 0) + (r_idx % HEAD_DIM) // n_blocks
    rank_sum = jnp.where((c_idx == r_bias_lane) & (r_idx % HEAD_DIM < n_pairs), 1.0, 0.0)
    rank_sum = rank_sum.astype(jnp.bfloat16)
    bias_blk = jnp.where(own0, lane, lane - HEAD_DIM)
    bias_lane = bias_blk < n_blocks

    def gate_stage(i):
        q_i = q_ref[i * MOBA_BLOCK:(i + 1) * MOBA_BLOCK, :]
        return _dot_nt(q_i * jnp.bfloat16(HEAD_DIM ** -0.5), gate_rhs)

    def rank_stage(i, gates):
        gate_j = gates[:, 0:LANES] + gates[:, LANES:2 * LANES]
        gate_jp = gates[:, 2 * LANES:3 * LANES] + gates[:, 3 * LANES:]
        beats = (((gate_jp > gate_j) | ((gate_jp == gate_j) & (trip_jp < trip_j)))
                 & (trip_jp < i) & trip_ok)
        return _dot(jnp.where(beats, 1.0, 0.0).astype(jnp.bfloat16), rank_sum)

    def score_stage(i, rank):
        keep = ((rank < MOBA_TOPK) & (bias_blk < i)) | (bias_blk == i) | ~bias_lane
        bias = jnp.where(keep, 0.0, MASK_NEG).astype(jnp.bfloat16)
        q_i = q_ref[i * MOBA_BLOCK:(i + 1) * MOBA_BLOCK, :]
        q_s = (q_i.astype(jnp.float32) * (HEAD_DIM ** -0.5 * LOG2_E)).astype(jnp.bfloat16)
        return (_scores(jnp.where(own0, q_s, bias), ka_ref.at[0], i, causal),
                _scores(jnp.where(own0, bias, q_s), ka_ref.at[1], i, causal))

    def value_stage(i, scores):
        out0 = _softmax_pv(scores[0], va_ref, i)
        out1 = _softmax_pv(scores[1], va_ref, i)
        o_ref[i * MOBA_BLOCK:(i + 1) * MOBA_BLOCK, :] = jnp.where(own0, out0, out1).astype(o_ref.dtype)

    order = [b for pair in zip(range(n_blocks - 1, -1, -1), range(n_blocks)) for b in pair][:n_blocks]
    gates, ranks, scores = {}, {}, {}
    for step in range(n_blocks + 3):
        if step < n_blocks:
            gates[order[step]] = gate_stage(order[step])
        if 1 <= step < n_blocks + 1:
            i = order[step - 1]
            ranks[i] = rank_stage(i, gates.pop(i))
        if 2 <= step < n_blocks + 2:
            i = order[step - 2]
            scores[i] = score_stage(i, ranks.pop(i))
        if step >= 3:
            i = order[step - 3]
            value_stage(i, scores.pop(i))


def _moba(q, k, v):
    n_batch, seq, d_attn = q.shape
    assert seq % MOBA_BLOCK == 0 and d_attn % LANES == 0
    n_blocks = seq // MOBA_BLOCK
    width = MOBA_SLABS * LANES
    assert d_attn % width == 0
    spec = pl.BlockSpec((None, seq, width), lambda b, s: (b, 0, s))
    return pl.pallas_call(
        functools.partial(_moba_kernel, n_blocks=n_blocks),
        out_shape=jax.ShapeDtypeStruct((n_batch, seq, d_attn), jnp.bfloat16),
        grid=(n_batch, d_attn // width),
        in_specs=[spec, spec, spec],
        out_specs=spec,
        scratch_shapes=[
            pltpu.VMEM((MOBA_SLABS, LANES // HEAD_DIM, seq, LANES), jnp.bfloat16),
            pltpu.VMEM((MOBA_SLABS, seq, 2 * LANES), jnp.bfloat16)],
        compiler_params=pltpu.CompilerParams(
            dimension_semantics=("arbitrary", "arbitrary"), vmem_limit_bytes=VMEM_LIMIT_BYTES),
        name="moba",
    )(q, k, v)


def _out_ffn_kernel(x_ref, ys_ref, ya_ref, gattn_ref, wos_ref, woa_ref, gpost_ref, gpre_ref,
                    wg_ref, wu_ref, wd_ref, gffn_ref, o_ref, acc_ref):
    ya = _rms(ya_ref[...].astype(jnp.float32), gattn_ref[...]).astype(jnp.bfloat16)
    mixed = _dot(ys_ref[...], wos_ref[...]) + _dot(ya, woa_ref[...])
    x1 = x_ref[...] + _rms(mixed, gpost_ref[...])
    h = _rms(x1, gpre_ref[...]).astype(jnp.bfloat16)
    n_chunks = wg_ref.shape[1] // FF_CHUNK

    def gate_up(c):
        lo, hi = c * FF_CHUNK, (c + 1) * FF_CHUNK
        return _dot(h, wg_ref[:, lo:hi]), _dot(h, wu_ref[:, lo:hi])

    pending = gate_up(0)
    for c in range(n_chunks):
        gate, up = pending
        if c + 1 < n_chunks:
            pending = gate_up(c + 1)
        f = (jax.nn.silu(gate) * up).astype(jnp.bfloat16)
        part = _dot(f, wd_ref[c * FF_CHUNK:(c + 1) * FF_CHUNK, :])
        if c == 0:
            acc_ref[...] = part
        else:
            acc_ref[...] += part
    o_ref[...] = x1 + _rms(acc_ref[...], gffn_ref[...])


def _out_ffn(x2, ys2, ya2, g_attn, wo_s, wo_a, g_post, g_pre, wg, wu, wd, g_ffn):
    n_rows, d_model = x2.shape
    d_half = ys2.shape[1]
    d_ff = wg.shape[1]
    assert n_rows % ROW_TILE == 0 and d_ff % FF_CHUNK == 0
    row = lambda i: (i, 0)
    resident = lambda a: pl.BlockSpec(a.shape, lambda i: (0,) * a.ndim, pipeline_mode=pl.Buffered(1))
    consts = (g_attn, wo_s, wo_a, g_post, g_pre, wg, wu, wd, g_ffn)
    return pl.pallas_call(
        _out_ffn_kernel,
        out_shape=jax.ShapeDtypeStruct((n_rows, d_model), jnp.float32),
        grid=(n_rows // ROW_TILE,),
        in_specs=[
            pl.BlockSpec((ROW_TILE, d_model), row),
            pl.BlockSpec((ROW_TILE, d_half), row),
            pl.BlockSpec((ROW_TILE, d_half), row),
        ] + [resident(a) for a in consts],
        out_specs=pl.BlockSpec((ROW_TILE, d_model), row),
        scratch_shapes=[pltpu.VMEM((ROW_TILE, d_model), jnp.float32)],
        compiler_params=pltpu.CompilerParams(
            dimension_semantics=("arbitrary",), vmem_limit_bytes=VMEM_LIMIT_BYTES),
        name="out_ffn",
    )(x2, ys2, ya2, *consts)


def _s5_params(a_re, a_im, log_dt, b_re, b_im, c_re, c_im, n_batch):
    f32 = jnp.float32
    a_re, a_im = a_re.astype(f32), a_im.astype(f32)
    b_re, b_im = b_re.astype(f32), b_im.astype(f32)
    n_groups = a_re.shape[0]
    n_clusters = n_groups // GROUPS_PER_CLUSTER
    dt = jnp.exp(log_dt.astype(f32))[:, None]
    mag = jnp.exp(a_re * dt)
    ang = a_im * dt
    lb_re = mag * jnp.cos(ang)
    lb_im = mag * jnp.sin(ang)
    nr = lb_re - 1.0
    den = a_re * a_re + a_im * a_im
    cr = (nr * a_re + lb_im * a_im) / den
    ci = (lb_im * a_re - nr * a_im) / den
    bb_re = cr[..., None] * b_re - ci[..., None] * b_im
    bb_im = cr[..., None] * b_im + ci[..., None] * b_re
    eye = jnp.eye(GROUPS_PER_CLUSTER, dtype=f32)

    def in_proj(bb):
        bb = bb.reshape(n_clusters, GROUPS_PER_CLUSTER, SSM_STATE, SSM_GROUP)
        m = jnp.einsum('cgph,gk->cghkp', bb, eye)
        return m.reshape(n_clusters, LANES, CLUSTER_STATE)

    def out_proj(cc):
        cc = cc.reshape(n_clusters, GROUPS_PER_CLUSTER, SSM_GROUP, SSM_STATE)
        m = jnp.einsum('cghp,gk->cgpkh', cc, eye)
        return m.reshape(n_clusters, CLUSTER_STATE, LANES)

    bbar = jnp.concatenate([in_proj(bb_re), in_proj(bb_im)], axis=2).astype(jnp.bfloat16)
    ctil = jnp.concatenate([out_proj(c_re.astype(f32)), -out_proj(c_im.astype(f32))],
                           axis=1).astype(jnp.bfloat16)
    lam_re = jnp.broadcast_to(lb_re.reshape(1, -1), (n_batch, lb_re.size))
    lam_im = jnp.broadcast_to(lb_im.reshape(1, -1), (n_batch, lb_im.size))
    return lam_re, lam_im, bbar, ctil


def _time_batch_permutation(n_batch):
    n = n_batch * PERM_T
    r = jnp.arange(n)
    src = (r % n_batch) * PERM_T + r // n_batch
    return (src[:, None] == jnp.arange(n)[None, :]).astype(jnp.bfloat16)


def _layer(x, g_pre_mix, w_in, ssm_a_re, ssm_a_im, ssm_log_dt, ssm_b_re, ssm_b_im, ssm_c_re,
           ssm_c_im, ssm_d, w_glu, b_glu, g_ssm_out, g_attn_out, w_out, g_post_mix, g_pre_ffn,
           w_gate, w_up, w_down, g_post_ffn):
    n_batch, seq, d_model = x.shape
    d_ssm = w_glu.shape[0]
    bf16 = jnp.bfloat16
    row = lambda g: g.reshape(1, -1).astype(jnp.float32)
    lam_re, lam_im, bbar, ctil = _s5_params(ssm_a_re, ssm_a_im, ssm_log_dt, ssm_b_re, ssm_b_im,
                                            ssm_c_re, ssm_c_im, n_batch)
    q, k, v, y_ssm = _mixer_in(
        x, row(g_pre_mix), w_in.astype(bf16), _time_batch_permutation(n_batch), lam_re, lam_im,
        bbar, ctil, row(ssm_d), w_glu.astype(bf16), row(b_glu), row(g_ssm_out))
    y_attn = _moba(q, k, v)
    n_rows = n_batch * seq
    out = _out_ffn(
        x.reshape(n_rows, d_model), y_ssm.reshape(n_rows, d_ssm), y_attn.reshape(n_rows, -1),
        row(g_attn_out), w_out[:d_ssm].astype(bf16), w_out[d_ssm:].astype(bf16), row(g_post_mix),
        row(g_pre_ffn), w_gate.astype(bf16), w_up.astype(bf16), w_down.astype(bf16),
        row(g_post_ffn))
    return out.reshape(n_batch, seq, d_model)


def kernel(x, g_pre_mix, w_in, ssm_a_re, ssm_a_im, ssm_log_dt, ssm_b_re, ssm_b_im, ssm_c_re, ssm_c_im, ssm_d, w_glu, b_glu, g_ssm_out, g_attn_out, w_out, g_post_mix, g_pre_ffn, w_gate, w_up, w_down, g_post_ffn):
    for l in range(w_in.shape[0]):
        x = _layer(x, g_pre_mix[l], w_in[l], ssm_a_re[l], ssm_a_im[l], ssm_log_dt[l], ssm_b_re[l],
                   ssm_b_im[l], ssm_c_re[l], ssm_c_im[l], ssm_d[l], w_glu[l], b_glu[l],
                   g_ssm_out[l], g_attn_out[l], w_out[l], g_post_mix[l], g_pre_ffn[l], w_gate[l],
                   w_up[l], w_down[l], g_post_ffn[l])
    return x
```

```python
import functools
import math

import jax
import jax.numpy as jnp
from jax import lax
from jax.experimental import pallas as pl
from jax.experimental.pallas import tpu as pltpu

SSM_GROUP = 16
SSM_STATE = 64
HEAD_DIM = 64
MOBA_BLOCK = 256
MOBA_TOPK = 3
RMS_EPS = 1e-6

LANES = 128
BF16_SUBLANES = 16
VMEM_LIMIT_BYTES = 56 * 1024 * 1024

GROUPS_PER_CLUSTER = LANES // SSM_GROUP
CLUSTER_STATE = GROUPS_PER_CLUSTER * SSM_STATE
TIME_TILE = 64
PERM_T = BF16_SUBLANES
MOBA_SLABS = 2
ROW_TILE = 1024
FF_CHUNK = 256
MASK_NEG = -1e30
LOG2_E = math.log2(math.e)

_NT = (((1,), (1,)), ((), ()))


def _rms(x, g):
    return x * lax.rsqrt(jnp.mean(x * x, axis=-1, keepdims=True) + RMS_EPS) * g


def _dot(a, b):
    return jnp.dot(a, b, preferred_element_type=jnp.float32)


def _dot_nt(a, b):
    return lax.dot_general(a, b, _NT, preferred_element_type=jnp.float32)


def _mixer_in_kernel(x_ref, gpre_ref, win_ref, perm_ref, lamre_ref, lamim_ref, bbar_ref, ctil_ref,
                     dskip_ref, wglu_ref, bglu_ref, gssm_ref,
                     q_ref, k_ref, v_ref, yssm_ref,
                     state_ref, utm_ref, bu_ref, s_ref, y_ref, *, n_batch, n_clusters):
    lt = x_ref.shape[1]
    d_model = x_ref.shape[2]
    d_ssm = utm_ref.shape[1]
    rows = n_batch * lt
    perm_rows = n_batch * PERM_T

    @pl.when(pl.program_id(0) == 0)
    def _():
        state_ref[...] = jnp.zeros_like(state_ref)

    x = x_ref[...].reshape(rows, d_model)
    h = _rms(x, gpre_ref[...]).astype(jnp.bfloat16)
    u = _dot(h, win_ref[:, 0:d_ssm]).astype(jnp.bfloat16)

    proj_cols = 2 * LANES
    projections = [(ref, col, tile) for ref, col in ((q_ref, 1), (k_ref, 2), (v_ref, 3))
                   for tile in range(d_ssm // proj_cols)]

    def project(count):
        for _ in range(min(count, len(projections))):
            ref, col, tile = projections.pop(0)
            c0 = col * d_ssm + tile * proj_cols
            z = _dot(h, win_ref[:, c0:c0 + proj_cols]).astype(jnp.bfloat16)
            ref[:, :, tile * proj_cols:(tile + 1) * proj_cols] = z.reshape(n_batch, lt, proj_cols)

    project(1)
    perm = perm_ref[...]
    for th in range(lt // PERM_T):
        chunk = jnp.concatenate(
            [u[b * lt + th * PERM_T: b * lt + (th + 1) * PERM_T] for b in range(n_batch)], axis=0)
        utm_ref[th * perm_rows:(th + 1) * perm_rows, :] = _dot(perm, chunk).astype(jnp.bfloat16)
    for c in range(n_clusters):
        bu_ref[c] = _dot(utm_ref[:, c * LANES:(c + 1) * LANES], bbar_ref[c])
        project(1)

    for c in range(n_clusters):
        lo, hi = c * CLUSTER_STATE, (c + 1) * CLUSTER_STATE
        lam_re = lamre_ref[:, lo:hi]
        lam_im = lamim_ref[:, lo:hi]
        h_re = state_ref[0, :, lo:hi]
        h_im = state_ref[1, :, lo:hi]
        for t in range(lt):
            r0 = t * n_batch
            b_re = bu_ref[c, r0:r0 + n_batch, 0:CLUSTER_STATE]
            b_im = bu_ref[c, r0:r0 + n_batch, CLUSTER_STATE:2 * CLUSTER_STATE]
            n_re = lam_re * h_re - lam_im * h_im + b_re
            n_im = lam_re * h_im + lam_im * h_re + b_im
            h_re, h_im = n_re, n_im
            s_ref[c, r0:r0 + n_batch, 0:CLUSTER_STATE] = n_re.astype(jnp.bfloat16)
            s_ref[c, r0:r0 + n_batch, CLUSTER_STATE:2 * CLUSTER_STATE] = n_im.astype(jnp.bfloat16)
        state_ref[0, :, lo:hi] = h_re
        state_ref[1, :, lo:hi] = h_im
        y_ref[:, c * LANES:(c + 1) * LANES] = _dot(s_ref[c], ctil_ref[c])

    y = y_ref[...] + dskip_ref[...] * utm_ref[...].astype(jnp.float32)
    y = jax.nn.gelu(y)
    glu = _dot(y.astype(jnp.bfloat16), wglu_ref[...])
    project(len(projections))
    y = y * jax.nn.sigmoid(glu + bglu_ref[...])
    yn = _rms(y, gssm_ref[...]).astype(jnp.bfloat16)
    for th in range(lt // PERM_T):
        back = _dot(perm, yn[th * perm_rows:(th + 1) * perm_rows]).astype(jnp.bfloat16)
        for b in range(n_batch):
            yssm_ref[b, th * PERM_T:(th + 1) * PERM_T, :] = back[b * PERM_T:(b + 1) * PERM_T]


def _mixer_in(x, g_pre, w_in, perm, lam_re, lam_im, bbar, ctil, d_skip, w_glu, b_glu, g_ssm):
    n_batch, seq, d_model = x.shape
    d_ssm = w_glu.shape[0]
    n_clusters = d_ssm // LANES
    lt = TIME_TILE
    rows = n_batch * lt
    assert seq % lt == 0 and lt % PERM_T == 0 and n_batch % BF16_SUBLANES == 0
    resident = lambda a: pl.BlockSpec(a.shape, lambda i: (0,) * a.ndim, pipeline_mode=pl.Buffered(1))
    act_spec = pl.BlockSpec((n_batch, lt, d_ssm), lambda i: (0, i, 0))
    act_shape = jax.ShapeDtypeStruct((n_batch, seq, d_ssm), jnp.bfloat16)
    consts = (g_pre, w_in, perm, lam_re, lam_im, bbar, ctil, d_skip, w_glu, b_glu, g_ssm)
    return pl.pallas_call(
        functools.partial(_mixer_in_kernel, n_batch=n_batch, n_clusters=n_clusters),
        out_shape=(act_shape,) * 4,
        grid=(seq // lt,),
        in_specs=[pl.BlockSpec((n_batch, lt, d_model), lambda i: (0, i, 0))]
        + [resident(a) for a in consts],
        out_specs=(act_spec,) * 4,
        scratch_shapes=[
            pltpu.VMEM((2, n_batch, n_clusters * CLUSTER_STATE), jnp.float32),
            pltpu.VMEM((rows, d_ssm), jnp.bfloat16),
            pltpu.VMEM((n_clusters, rows, 2 * CLUSTER_STATE), jnp.float32),
            pltpu.VMEM((n_clusters, rows, 2 * CLUSTER_STATE), jnp.bfloat16),
            pltpu.VMEM((rows, d_ssm), jnp.float32),
        ],
        compiler_params=pltpu.CompilerParams(
            dimension_semantics=("arbitrary",), vmem_limit_bytes=VMEM_LIMIT_BYTES),
        name="mixer_in",
    )(x, *consts)


def _scores(q_aug, ka_ref, i, causal):
    r0 = i * MOBA_BLOCK
    s_own = jnp.where(causal, _dot_nt(q_aug, ka_ref[r0:r0 + MOBA_BLOCK, :]), MASK_NEG)
    s_past = _dot_nt(q_aug, ka_ref[0:r0, :]) if i > 0 else None
    return s_own, s_past


def _softmax_pv(scores, va_ref, i):
    s_own, s_past = scores
    r0 = i * MOBA_BLOCK
    m = jnp.max(s_own, axis=-1, keepdims=True)
    if s_past is not None:
        m = jnp.maximum(m, jnp.max(s_past, axis=-1, keepdims=True))
    acc = _dot(jnp.exp2(s_own - m).astype(jnp.bfloat16), va_ref[r0:r0 + MOBA_BLOCK, :])
    if s_past is not None:
        acc = acc + _dot(jnp.exp2(s_past - m).astype(jnp.bfloat16), va_ref[0:r0, :])
    return acc[:, :LANES] / acc[:, LANES:]


def _moba_kernel(q_ref, k_ref, v_ref, o_ref, ka_ref, va_ref, *, n_blocks):
    stages = []
    for s in range(q_ref.shape[1] // LANES):
        cols = slice(s * LANES, (s + 1) * LANES)
        stages.append(_moba_slab(q_ref.at[:, cols], k_ref.at[:, cols], v_ref.at[:, cols],
                                 o_ref.at[:, cols], ka_ref.at[s], va_ref.at[s], n_blocks))

    order = [b for pair in zip(range(n_blocks - 1, -1, -1), range(n_blocks)) for b in pair][:n_blocks]
    items = [(s, i) for s in range(len(stages)) for i in order]
    results = [{}, {}, {}]
    for step in range(len(items) + 3):
        for depth in range(4):
            if 0 <= step - depth < len(items):
                item = items[step - depth]
                args = (results[depth - 1].pop(item),) if depth else ()
                out = stages[item[0]][depth](item[1], *args)
                if depth < 3:
                    results[depth][item] = out


def _moba_slab(q_ref, k_ref, v_ref, o_ref, ka_ref, va_ref, n_blocks):
    seq = q_ref.shape[0]
    n_pairs = n_blocks * n_blocks
    assert LANES == 2 * HEAD_DIM and n_pairs <= HEAD_DIM
    lane = lax.broadcasted_iota(jnp.int32, (1, LANES), 1)
    own0 = lane < HEAD_DIM
    causal = (lax.broadcasted_iota(jnp.int32, (MOBA_BLOCK, MOBA_BLOCK), 1)
              <= lax.broadcasted_iota(jnp.int32, (MOBA_BLOCK, MOBA_BLOCK), 0))

    key_blk = lax.broadcasted_iota(jnp.int32, (seq, 1), 0) // MOBA_BLOCK
    lane_full = lax.broadcasted_iota(jnp.int32, (seq, LANES), 1)
    k_all = k_ref[...]
    ka_ref[0] = jnp.where(own0, k_all,
                          jnp.where(lane_full == key_blk + HEAD_DIM, 1.0, 0.0).astype(jnp.bfloat16))
    ka_ref[1] = jnp.where(own0, jnp.where(lane_full == key_blk, 1.0, 0.0).astype(jnp.bfloat16), k_all)
    va_ref[:, :LANES] = v_ref[...]
    va_ref[:, LANES:] = jnp.ones((seq, LANES), jnp.bfloat16)

    k_mean = jnp.sum(k_all.astype(jnp.float32).reshape(n_blocks, MOBA_BLOCK, LANES), axis=1)
    k_mean = k_mean * (1.0 / MOBA_BLOCK)
    km_a = jnp.broadcast_to(k_mean[:, None, :], (n_blocks, n_blocks, LANES)).reshape(n_pairs, LANES)
    km_b = jnp.broadcast_to(k_mean[None, :, :], (n_blocks, n_blocks, LANES)).reshape(n_pairs, LANES)
    pad = [jnp.zeros((HEAD_DIM - n_pairs, LANES), jnp.float32)] if n_pairs < HEAD_DIM else []

    def both_heads(km):
        return jnp.concatenate(
            [jnp.where(own0, km, 0.0)] + pad + [jnp.where(own0, 0.0, km)] + pad, axis=0)

    def split(m):
        hi = m.astype(jnp.bfloat16)
        return [hi, (m - hi.astype(jnp.float32)).astype(jnp.bfloat16)]
    gate_rhs = jnp.concatenate(split(both_heads(km_a)) + split(both_heads(km_b)), axis=0)

    trip = lane % HEAD_DIM
    trip_j, trip_jp = trip // n_blocks, trip % n_blocks
    trip_ok = trip < n_pairs
    r_idx = lax.broadcasted_iota(jnp.int32, (LANES, LANES), 0)
    c_idx = lax.broadcasted_iota(jnp.int32, (LANES, LANES), 1)
    r_bias_lane = jnp.where(r_idx < HEAD_DIM, HEAD_DIM, 0) + (r_idx % HEAD_DIM) // n_blocks
    rank_sum = jnp.where((c_idx == r_bias_lane) & (r_idx % HEAD_DIM < n_pairs), 1.0, 0.0)
    rank_sum = rank_sum.astype(jnp.bfloat16)
    bias_blk = jnp.where(own0, lane, lane - HEAD_DIM)
    bias_lane = bias_blk < n_blocks

    def gate_stage(i):
        q_i = q_ref[i * MOBA_BLOCK:(i + 1) * MOBA_BLOCK, :]
        return _dot_nt(q_i * jnp.bfloat16(HEAD_DIM ** -0.5), gate_rhs)

    def rank_stage(i, gates):
        gate_j = gates[:, 0:LANES] + gates[:, LANES:2 * LANES]
        gate_jp = gates[:, 2 * LANES:3 * LANES] + gates[:, 3 * LANES:]
        beats = (((gate_jp > gate_j) | ((gate_jp == gate_j) & (trip_jp < trip_j)))
                 & (trip_jp < i) & trip_ok)
        return _dot(jnp.where(beats, 1.0, 0.0).astype(jnp.bfloat16), rank_sum)

    def score_stage(i, rank):
        keep = ((rank < MOBA_TOPK) & (bias_blk < i)) | (bias_blk == i) | ~bias_lane
        bias = jnp.where(keep, 0.0, MASK_NEG).astype(jnp.bfloat16)
        q_i = q_ref[i * MOBA_BLOCK:(i + 1) * MOBA_BLOCK, :]
        q_s = (q_i.astype(jnp.float32) * (HEAD_DIM ** -0.5 * LOG2_E)).astype(jnp.bfloat16)
        return (_scores(jnp.where(own0, q_s, bias), ka_ref.at[0], i, causal),
                _scores(jnp.where(own0, bias, q_s), ka_ref.at[1], i, causal))

    def value_stage(i, scores):
        out0 = _softmax_pv(scores[0], va_ref, i)
        out1 = _softmax_pv(scores[1], va_ref, i)
        o_ref[i * MOBA_BLOCK:(i + 1) * MOBA_BLOCK, :] = jnp.where(own0, out0, out1).astype(o_ref.dtype)

    return gate_stage, rank_stage, score_stage, value_stage


def _moba(q, k, v):
    n_batch, seq, d_attn = q.shape
    assert seq % MOBA_BLOCK == 0 and d_attn % LANES == 0
    n_blocks = seq // MOBA_BLOCK
    width = MOBA_SLABS * LANES
    assert d_attn % width == 0
    spec = pl.BlockSpec((None, seq, width), lambda b, s: (b, 0, s))
    return pl.pallas_call(
        functools.partial(_moba_kernel, n_blocks=n_blocks),
        out_shape=jax.ShapeDtypeStruct((n_batch, seq, d_attn), jnp.bfloat16),
        grid=(n_batch, d_attn // width),
        in_specs=[spec, spec, spec],
        out_specs=spec,
        scratch_shapes=[
            pltpu.VMEM((MOBA_SLABS, LANES // HEAD_DIM, seq, LANES), jnp.bfloat16),
            pltpu.VMEM((MOBA_SLABS, seq, 2 * LANES), jnp.bfloat16)],
        compiler_params=pltpu.CompilerParams(
            dimension_semantics=("arbitrary", "arbitrary"), vmem_limit_bytes=VMEM_LIMIT_BYTES),
        name="moba",
    )(q, k, v)


def _out_ffn_kernel(x_ref, ys_ref, ya_ref, gattn_ref, wos_ref, woa_ref, gpost_ref, gpre_ref,
                    wg_ref, wu_ref, wd_ref, gffn_ref, o_ref, acc_ref):
    n_chunks = wg_ref.shape[1] // FF_CHUNK
    half = x_ref.shape[0] // 2

    def mix_stage(r):
        rows = slice(r * half, (r + 1) * half)
        ya = _rms(ya_ref[rows, :].astype(jnp.float32), gattn_ref[...]).astype(jnp.bfloat16)
        mixed = _dot(ys_ref[rows, :], wos_ref[...]) + _dot(ya, woa_ref[...])
        x1 = x_ref[rows, :] + _rms(mixed, gpost_ref[...])
        return x1, _rms(x1, gpre_ref[...]).astype(jnp.bfloat16)

    def gate_up(h, c):
        lo, hi = c * FF_CHUNK, (c + 1) * FF_CHUNK
        return _dot(h, wg_ref[:, lo:hi]), _dot(h, wu_ref[:, lo:hi])

    def ffn_chunk(r, h, c, pending):
        gate, up = pending
        pending = gate_up(h, c + 1) if c + 1 < n_chunks else None
        f = (jax.nn.silu(gate) * up).astype(jnp.bfloat16)
        part = _dot(f, wd_ref[c * FF_CHUNK:(c + 1) * FF_CHUNK, :])
        if c == 0:
            acc_ref[r] = part
        else:
            acc_ref[r] += part
        return pending

    def out_stage(r, x1):
        o_ref[r * half:(r + 1) * half, :] = x1 + _rms(acc_ref[r], gffn_ref[...])

    x1_a, h_a = mix_stage(0)
    x1_b, h_b = mix_stage(1)
    pending = gate_up(h_a, 0)
    for c in range(n_chunks):
        pending = ffn_chunk(0, h_a, c, pending)
    pending = gate_up(h_b, 0)
    for c in range(n_chunks):
        pending = ffn_chunk(1, h_b, c, pending)
        if c == 0:
            out_stage(0, x1_a)
    out_stage(1, x1_b)


def _out_ffn(x2, ys2, ya2, g_attn, wo_s, wo_a, g_post, g_pre, wg, wu, wd, g_ffn):
    n_rows, d_model = x2.shape
    d_half = ys2.shape[1]
    d_ff = wg.shape[1]
    assert n_rows % ROW_TILE == 0 and d_ff % FF_CHUNK == 0
    row = lambda i: (i, 0)
    resident = lambda a: pl.BlockSpec(a.shape, lambda i: (0,) * a.ndim, pipeline_mode=pl.Buffered(1))
    consts = (g_attn, wo_s, wo_a, g_post, g_pre, wg, wu, wd, g_ffn)
    return pl.pallas_call(
        _out_ffn_kernel,
        out_shape=jax.ShapeDtypeStruct((n_rows, d_model), jnp.float32),
        grid=(n_rows // ROW_TILE,),
        in_specs=[
            pl.BlockSpec((ROW_TILE, d_model), row),
            pl.BlockSpec((ROW_TILE, d_half), row),
            pl.BlockSpec((ROW_TILE, d_half), row),
        ] + [resident(a) for a in consts],
        out_specs=pl.BlockSpec((ROW_TILE, d_model), row),
        scratch_shapes=[pltpu.VMEM((2, ROW_TILE // 2, d_model), jnp.float32)],
        compiler_params=pltpu.CompilerParams(
            dimension_semantics=("arbitrary",), vmem_limit_bytes=VMEM_LIMIT_BYTES),
        name="out_ffn",
    )(x2, ys2, ya2, *consts)


def _s5_params(a_re, a_im, log_dt, b_re, b_im, c_re, c_im, n_batch):
    f32 = jnp.float32
    a_re, a_im = a_re.astype(f32), a_im.astype(f32)
    b_re, b_im = b_re.astype(f32), b_im.astype(f32)
    n_groups = a_re.shape[0]
    n_clusters = n_groups // GROUPS_PER_CLUSTER
    dt = jnp.exp(log_dt.astype(f32))[:, None]
    mag = jnp.exp(a_re * dt)
    ang = a_im * dt
    lb_re = mag * jnp.cos(ang)
    lb_im = mag * jnp.sin(ang)
    nr = lb_re - 1.0
    den = a_re * a_re + a_im * a_im
    cr = (nr * a_re + lb_im * a_im) / den
    ci = (lb_im * a_re - nr * a_im) / den
    bb_re = cr[..., None] * b_re - ci[..., None] * b_im
    bb_im = cr[..., None] * b_im + ci[..., None] * b_re
    eye = jnp.eye(GROUPS_PER_CLUSTER, dtype=f32)

    def in_proj(bb):
        bb = bb.reshape(n_clusters, GROUPS_PER_CLUSTER, SSM_STATE, SSM_GROUP)
        m = jnp.einsum('cgph,gk->cghkp', bb, eye)
        return m.reshape(n_clusters, LANES, CLUSTER_STATE)

    def out_proj(cc):
        cc = cc.reshape(n_clusters, GROUPS_PER_CLUSTER, SSM_GROUP, SSM_STATE)
        m = jnp.einsum('cghp,gk->cgpkh', cc, eye)
        return m.reshape(n_clusters, CLUSTER_STATE, LANES)

    bbar = jnp.concatenate([in_proj(bb_re), in_proj(bb_im)], axis=2).astype(jnp.bfloat16)
    ctil = jnp.concatenate([out_proj(c_re.astype(f32)), -out_proj(c_im.astype(f32))],
                           axis=1).astype(jnp.bfloat16)
    lam_re = jnp.broadcast_to(lb_re.reshape(1, -1), (n_batch, lb_re.size))
    lam_im = jnp.broadcast_to(lb_im.reshape(1, -1), (n_batch, lb_im.size))
    return lam_re, lam_im, bbar, ctil


def _time_batch_permutation(n_batch):
    n = n_batch * PERM_T
    r = jnp.arange(n)
    src = (r % n_batch) * PERM_T + r // n_batch
    return (src[:, None] == jnp.arange(n)[None, :]).astype(jnp.bfloat16)


def _layer(x, g_pre_mix, w_in, ssm_a_re, ssm_a_im, ssm_log_dt, ssm_b_re, ssm_b_im, ssm_c_re,
           ssm_c_im, ssm_d, w_glu, b_glu, g_ssm_out, g_attn_out, w_out, g_post_mix, g_pre_ffn,
           w_gate, w_up, w_down, g_post_ffn):
    n_batch, seq, d_model = x.shape
    d_ssm = w_glu.shape[0]
    bf16 = jnp.bfloat16
    row = lambda g: g.reshape(1, -1).astype(jnp.float32)
    lam_re, lam_im, bbar, ctil = _s5_params(ssm_a_re, ssm_a_im, ssm_log_dt, ssm_b_re, ssm_b_im,
                                            ssm_c_re, ssm_c_im, n_batch)
    q, k, v, y_ssm = _mixer_in(
        x, row(g_pre_mix), w_in.astype(bf16), _time_batch_permutation(n_batch), lam_re, lam_im,
        bbar, ctil, row(ssm_d), w_glu.astype(bf16), row(b_glu), row(g_ssm_out))
    y_attn = _moba(q, k, v)
    n_rows = n_batch * seq
    out = _out_ffn(
        x.reshape(n_rows, d_model), y_ssm.reshape(n_rows, d_ssm), y_attn.reshape(n_rows, -1),
        row(g_attn_out), w_out[:d_ssm].astype(bf16), w_out[d_ssm:].astype(bf16), row(g_post_mix),
        row(g_pre_ffn), w_gate.astype(bf16), w_up.astype(bf16), w_down.astype(bf16),
        row(g_post_ffn))
    return out.reshape(n_batch, seq, d_model)


def kernel(x, g_pre_mix, w_in, ssm_a_re, ssm_a_im, ssm_log_dt, ssm_b_re, ssm_b_im, ssm_c_re, ssm_c_im, ssm_d, w_glu, b_glu, g_ssm_out, g_attn_out, w_out, g_post_mix, g_pre_ffn, w_gate, w_up, w_down, g_post_ffn):
    for l in range(w_in.shape[0]):
        x = _layer(x, g_pre_mix[l], w_in[l], ssm_a_re[l], ssm_a_im[l], ssm_log_dt[l], ssm_b_re[l],
                   ssm_b_im[l], ssm_c_re[l], ssm_c_im[l], ssm_d[l], w_glu[l], b_glu[l],
                   g_ssm_out[l], g_attn_out[l], w_out[l], g_post_mix[l], g_pre_ffn[l], w_gate[l],
                   w_up[l], w_down[l], g_post_ffn[l])
    return x
```

```python
import collections
import functools
import math

import jax
import jax.numpy as jnp
from jax import lax
from jax.experimental import pallas as pl
from jax.experimental.pallas import tpu as pltpu

SSM_GROUP = 16
SSM_STATE = 64
HEAD_DIM = 64
MOBA_BLOCK = 256
MOBA_TOPK = 3
RMS_EPS = 1e-6

LANES = 128
BF16_SUBLANES = 16
VMEM_LIMIT_BYTES = 56 * 1024 * 1024

GROUPS_PER_CLUSTER = LANES // SSM_GROUP
CLUSTER_STATE = GROUPS_PER_CLUSTER * SSM_STATE
TIME_TILE = 64
PERM_T = BF16_SUBLANES
MOBA_SLABS = 2
ROW_TILE = 1024
FF_CHUNK = 256
MASK_NEG = -1e30
LOG2_E = math.log2(math.e)

_NT = (((1,), (1,)), ((), ()))
_SlabStages = collections.namedtuple('_SlabStages', 'gate_stage rank_stage score_stage value_stage')


def _rms(x, g):
    return x * lax.rsqrt(jnp.mean(x * x, axis=-1, keepdims=True) + RMS_EPS) * g


def _dot(a, b):
    return jnp.dot(a, b, preferred_element_type=jnp.float32)


def _dot_nt(a, b):
    return lax.dot_general(a, b, _NT, preferred_element_type=jnp.float32)


def _mixer_in_kernel(x_ref, gpre_ref, win_ref, perm_ref, lamre_ref, lamim_ref, bbar_ref, ctil_ref,
                     dskip_ref, wglu_ref, bglu_ref, gssm_ref,
                     q_ref, k_ref, v_ref, yssm_ref,
                     state_ref, utm_ref, bu_ref, s_ref, y_ref, *, n_batch, n_clusters):
    lt = x_ref.shape[1]
    d_model = x_ref.shape[2]
    d_ssm = utm_ref.shape[1]
    rows = n_batch * lt
    perm_rows = n_batch * PERM_T

    @pl.when(pl.program_id(0) == 0)
    def _():
        state_ref[...] = jnp.zeros_like(state_ref)

    x = x_ref[...].reshape(rows, d_model)
    h = _rms(x, gpre_ref[...]).astype(jnp.bfloat16)
    u = _dot(h, win_ref[:, 0:d_ssm]).astype(jnp.bfloat16)

    proj_cols = 2 * LANES
    projections = [(ref, col, tile) for ref, col in ((q_ref, 1), (k_ref, 2), (v_ref, 3))
                   for tile in range(d_ssm // proj_cols)]

    def project(count):
        for _ in range(min(count, len(projections))):
            ref, col, tile = projections.pop(0)
            c0 = col * d_ssm + tile * proj_cols
            z = _dot(h, win_ref[:, c0:c0 + proj_cols]).astype(jnp.bfloat16)
            ref[:, :, tile * proj_cols:(tile + 1) * proj_cols] = z.reshape(n_batch, lt, proj_cols)

    project(1)
    perm = perm_ref[...]
    for th in range(lt // PERM_T):
        chunk = jnp.concatenate(
            [u[b * lt + th * PERM_T: b * lt + (th + 1) * PERM_T] for b in range(n_batch)], axis=0)
        utm_ref[th * perm_rows:(th + 1) * perm_rows, :] = _dot(perm, chunk).astype(jnp.bfloat16)
    for c in range(n_clusters):
        bu_ref[c] = _dot(utm_ref[:, c * LANES:(c + 1) * LANES], bbar_ref[c])
        project(1)

    for c in range(n_clusters):
        lo, hi = c * CLUSTER_STATE, (c + 1) * CLUSTER_STATE
        lam_re = lamre_ref[:, lo:hi]
        lam_im = lamim_ref[:, lo:hi]
        h_re = state_ref[0, :, lo:hi]
        h_im = state_ref[1, :, lo:hi]
        for t in range(lt):
            r0 = t * n_batch
            b_re = bu_ref[c, r0:r0 + n_batch, 0:CLUSTER_STATE]
            b_im = bu_ref[c, r0:r0 + n_batch, CLUSTER_STATE:2 * CLUSTER_STATE]
            n_re = lam_re * h_re - lam_im * h_im + b_re
            n_im = lam_re * h_im + lam_im * h_re + b_im
            h_re, h_im = n_re, n_im
            s_ref[c, r0:r0 + n_batch, 0:CLUSTER_STATE] = n_re.astype(jnp.bfloat16)
            s_ref[c, r0:r0 + n_batch, CLUSTER_STATE:2 * CLUSTER_STATE] = n_im.astype(jnp.bfloat16)
        state_ref[0, :, lo:hi] = h_re
        state_ref[1, :, lo:hi] = h_im
        y_ref[:, c * LANES:(c + 1) * LANES] = _dot(s_ref[c], ctil_ref[c])

    y = y_ref[...] + dskip_ref[...] * utm_ref[...].astype(jnp.float32)
    y = jax.nn.gelu(y)
    glu = _dot(y.astype(jnp.bfloat16), wglu_ref[...])
    project(len(projections))
    y = y * jax.nn.sigmoid(glu + bglu_ref[...])
    yn = _rms(y, gssm_ref[...]).astype(jnp.bfloat16)
    for th in range(lt // PERM_T):
        back = _dot(perm, yn[th * perm_rows:(th + 1) * perm_rows]).astype(jnp.bfloat16)
        for b in range(n_batch):
            yssm_ref[b, th * PERM_T:(th + 1) * PERM_T, :] = back[b * PERM_T:(b + 1) * PERM_T]


def _mixer_in(x, g_pre, w_in, perm, lam_re, lam_im, bbar, ctil, d_skip, w_glu, b_glu, g_ssm):
    n_batch, seq, d_model = x.shape
    d_ssm = w_glu.shape[0]
    n_clusters = d_ssm // LANES
    lt = TIME_TILE
    rows = n_batch * lt
    assert seq % lt == 0 and lt % PERM_T == 0 and n_batch % BF16_SUBLANES == 0
    resident = lambda a: pl.BlockSpec(a.shape, lambda i: (0,) * a.ndim, pipeline_mode=pl.Buffered(1))
    act_spec = pl.BlockSpec((n_batch, lt, d_ssm), lambda i: (0, i, 0))
    act_shape = jax.ShapeDtypeStruct((n_batch, seq, d_ssm), jnp.bfloat16)
    consts = (g_pre, w_in, perm, lam_re, lam_im, bbar, ctil, d_skip, w_glu, b_glu, g_ssm)
    return pl.pallas_call(
        functools.partial(_mixer_in_kernel, n_batch=n_batch, n_clusters=n_clusters),
        out_shape=(act_shape,) * 4,
        grid=(seq // lt,),
        in_specs=[pl.BlockSpec((n_batch, lt, d_model), lambda i: (0, i, 0))]
        + [resident(a) for a in consts],
        out_specs=(act_spec,) * 4,
        scratch_shapes=[
            pltpu.VMEM((2, n_batch, n_clusters * CLUSTER_STATE), jnp.float32),
            pltpu.VMEM((rows, d_ssm), jnp.bfloat16),
            pltpu.VMEM((n_clusters, rows, 2 * CLUSTER_STATE), jnp.float32),
            pltpu.VMEM((n_clusters, rows, 2 * CLUSTER_STATE), jnp.bfloat16),
            pltpu.VMEM((rows, d_ssm), jnp.float32),
        ],
        compiler_params=pltpu.CompilerParams(
            dimension_semantics=("arbitrary",), vmem_limit_bytes=VMEM_LIMIT_BYTES),
        name="mixer_in",
    )(x, *consts)


def _block_order(i):
    return [i] + list(range(i))


def _score_ops(q_aug_t, ka_ref, i, causal, out):
    for j in _block_order(i):
        s = _dot(ka_ref[j * MOBA_BLOCK:(j + 1) * MOBA_BLOCK, :], q_aug_t)
        out.append(jnp.where(causal, s, MASK_NEG) if j == i else s)
        yield


def _value_ops(scores, vt_ref, i, out):
    m = functools.reduce(jnp.maximum, [jnp.max(s, axis=0, keepdims=True) for s in scores])
    acc = None
    for j, s in zip(_block_order(i), scores):
        part = _dot(vt_ref[:, j * MOBA_BLOCK:(j + 1) * MOBA_BLOCK],
                    jnp.exp2(s - m).astype(jnp.bfloat16))
        acc = part if acc is None else acc + part
        yield
    out.append(acc[:HEAD_DIM] / acc[HEAD_DIM:HEAD_DIM + 1])


def _interleave(*generators):
    live = list(generators)
    while live:
        for g in list(live):
            try:
                next(g)
            except StopIteration:
                live.remove(g)


def _moba_kernel(q_ref, k_ref, v_ref, o_ref, ka_ref, vt_ref, *, n_blocks):
    stages = []
    for s in range(q_ref.shape[1] // LANES):
        cols = slice(s * LANES, (s + 1) * LANES)
        stages.append(_moba_slab(q_ref.at[:, cols], k_ref.at[:, cols], v_ref.at[:, cols],
                                 o_ref.at[:, cols], ka_ref.at[s], vt_ref.at[s], n_blocks))

    items = [(s, i) for s in range(len(stages)) for i in range(n_blocks - 1, -1, -1)]
    gates, ranks, scores = {}, {}, {}
    for step in range(len(items) + 3):
        if step < len(items):
            s, i = items[step]
            gates[s, i] = stages[s].gate_stage(i)
        if 0 <= step - 1 < len(items):
            s, i = items[step - 1]
            ranks[s, i] = stages[s].rank_stage(i, gates.pop((s, i)))
        pending = []
        if 0 <= step - 2 < len(items):
            s, i = items[step - 2]
            scores[s, i] = ([], [])
            pending.append(stages[s].score_stage(i, ranks.pop((s, i)), scores[s, i]))
        if 0 <= step - 3 < len(items):
            s, i = items[step - 3]
            pending.append(stages[s].value_stage(i, scores.pop((s, i))))
        _interleave(*pending)


def _moba_slab(q_ref, k_ref, v_ref, o_ref, ka_ref, vt_ref, n_blocks):
    seq = q_ref.shape[0]
    n_pairs = n_blocks * n_blocks
    assert LANES == 2 * HEAD_DIM and n_pairs <= HEAD_DIM
    lane = lax.broadcasted_iota(jnp.int32, (1, LANES), 1)
    own0 = lane < HEAD_DIM
    causal = (lax.broadcasted_iota(jnp.int32, (MOBA_BLOCK, MOBA_BLOCK), 0)
              <= lax.broadcasted_iota(jnp.int32, (MOBA_BLOCK, MOBA_BLOCK), 1))

    key_blk = lax.broadcasted_iota(jnp.int32, (seq, 1), 0) // MOBA_BLOCK
    lane_full = lax.broadcasted_iota(jnp.int32, (seq, LANES), 1)
    k_all = k_ref[...]
    ka_ref[0] = jnp.where(own0, k_all,
                          jnp.where(lane_full == key_blk + HEAD_DIM, 1.0, 0.0).astype(jnp.bfloat16))
    ka_ref[1] = jnp.where(own0, jnp.where(lane_full == key_blk, 1.0, 0.0).astype(jnp.bfloat16), k_all)
    v_t = v_ref[...].T
    ones_rows = jnp.where(lax.broadcasted_iota(jnp.int32, (BF16_SUBLANES, seq), 0) == 0, 1.0, 0.0)
    ones_rows = ones_rows.astype(jnp.bfloat16)
    vt_ref[0] = jnp.concatenate([v_t[:HEAD_DIM], ones_rows], axis=0)
    vt_ref[1] = jnp.concatenate([v_t[HEAD_DIM:], ones_rows], axis=0)

    k_mean = jnp.sum(k_all.astype(jnp.float32).reshape(n_blocks, MOBA_BLOCK, LANES), axis=1)
    k_mean = k_mean * (1.0 / MOBA_BLOCK)
    km_a = jnp.broadcast_to(k_mean[:, None, :], (n_blocks, n_blocks, LANES)).reshape(n_pairs, LANES)
    km_b = jnp.broadcast_to(k_mean[None, :, :], (n_blocks, n_blocks, LANES)).reshape(n_pairs, LANES)
    pad = [jnp.zeros((HEAD_DIM - n_pairs, LANES), jnp.float32)] if n_pairs < HEAD_DIM else []

    def both_heads(km):
        return jnp.concatenate(
            [jnp.where(own0, km, 0.0)] + pad + [jnp.where(own0, 0.0, km)] + pad, axis=0)

    def split(m):
        hi = m.astype(jnp.bfloat16)
        return [hi, (m - hi.astype(jnp.float32)).astype(jnp.bfloat16)]
    gate_rhs = jnp.concatenate(split(both_heads(km_a)) + split(both_heads(km_b)), axis=0)

    trip = lane % HEAD_DIM
    trip_j, trip_jp = trip // n_blocks, trip % n_blocks
    trip_ok = trip < n_pairs
    r_idx = lax.broadcasted_iota(jnp.int32, (LANES, LANES), 0)
    c_idx = lax.broadcasted_iota(jnp.int32, (LANES, LANES), 1)
    r_bias_lane = jnp.where(r_idx < HEAD_DIM, HEAD_DIM, 0) + (r_idx % HEAD_DIM) // n_blocks
    rank_sum = jnp.where((c_idx == r_bias_lane) & (r_idx % HEAD_DIM < n_pairs), 1.0, 0.0)
    rank_sum = rank_sum.astype(jnp.bfloat16)
    bias_blk = jnp.where(own0, lane, lane - HEAD_DIM)
    bias_lane = bias_blk < n_blocks

    def gate_stage(i):
        q_i = q_ref[i * MOBA_BLOCK:(i + 1) * MOBA_BLOCK, :]
        return _dot_nt(q_i * jnp.bfloat16(HEAD_DIM ** -0.5), gate_rhs)

    def rank_stage(i, gates):
        gate_j = gates[:, 0:LANES] + gates[:, LANES:2 * LANES]
        gate_jp = gates[:, 2 * LANES:3 * LANES] + gates[:, 3 * LANES:]
        beats = (((gate_jp > gate_j) | ((gate_jp == gate_j) & (trip_jp < trip_j)))
                 & (trip_jp < i) & trip_ok)
        return _dot(jnp.where(beats, 1.0, 0.0).astype(jnp.bfloat16), rank_sum)

    def score_stage(i, rank, out):
        keep = ((rank < MOBA_TOPK) & (bias_blk < i)) | (bias_blk == i) | ~bias_lane
        bias = jnp.where(keep, 0.0, MASK_NEG).astype(jnp.bfloat16)
        q_i = q_ref[i * MOBA_BLOCK:(i + 1) * MOBA_BLOCK, :]
        q_s = (q_i.astype(jnp.float32) * (HEAD_DIM ** -0.5 * LOG2_E)).astype(jnp.bfloat16)
        yield from _score_ops(jnp.where(own0, q_s, bias).T, ka_ref.at[0], i, causal, out[0])
        yield from _score_ops(jnp.where(own0, bias, q_s).T, ka_ref.at[1], i, causal, out[1])

    def value_stage(i, scores):
        out = []
        yield from _value_ops(scores[0], vt_ref.at[0], i, out)
        yield from _value_ops(scores[1], vt_ref.at[1], i, out)
        o_ref[i * MOBA_BLOCK:(i + 1) * MOBA_BLOCK, :] = (
            jnp.concatenate(out, axis=0).T.astype(o_ref.dtype))

    return _SlabStages(gate_stage, rank_stage, score_stage, value_stage)


def _moba(q, k, v):
    n_batch, seq, d_attn = q.shape
    assert seq % MOBA_BLOCK == 0 and d_attn % LANES == 0
    n_blocks = seq // MOBA_BLOCK
    width = MOBA_SLABS * LANES
    assert d_attn % width == 0
    spec = pl.BlockSpec((None, seq, width), lambda b, s: (b, 0, s))
    return pl.pallas_call(
        functools.partial(_moba_kernel, n_blocks=n_blocks),
        out_shape=jax.ShapeDtypeStruct((n_batch, seq, d_attn), jnp.bfloat16),
        grid=(n_batch, d_attn // width),
        in_specs=[spec, spec, spec],
        out_specs=spec,
        scratch_shapes=[
            pltpu.VMEM((MOBA_SLABS, LANES // HEAD_DIM, seq, LANES), jnp.bfloat16),
            pltpu.VMEM((MOBA_SLABS, LANES // HEAD_DIM, HEAD_DIM + BF16_SUBLANES, seq),
                       jnp.bfloat16)],
        compiler_params=pltpu.CompilerParams(
            dimension_semantics=("arbitrary", "arbitrary"), vmem_limit_bytes=VMEM_LIMIT_BYTES),
        name="moba",
    )(q, k, v)


def _out_ffn_kernel(x_ref, ys_ref, ya_ref, gattn_ref, wos_ref, woa_ref, gpost_ref, gpre_ref,
                    wg_ref, wu_ref, wd_ref, gffn_ref, o_ref, acc_ref):
    n_chunks = wg_ref.shape[1] // FF_CHUNK
    half = x_ref.shape[0] // 2

    def mix_stage(r):
        rows = slice(r * half, (r + 1) * half)
        ya = _rms(ya_ref[rows, :].astype(jnp.float32), gattn_ref[...]).astype(jnp.bfloat16)
        mixed = _dot(ys_ref[rows, :], wos_ref[...]) + _dot(ya, woa_ref[...])
        x1 = x_ref[rows, :] + _rms(mixed, gpost_ref[...])
        return x1, _rms(x1, gpre_ref[...]).astype(jnp.bfloat16)

    def gate_up(h, c):
        lo, hi = c * FF_CHUNK, (c + 1) * FF_CHUNK
        return _dot(h, wg_ref[:, lo:hi]), _dot(h, wu_ref[:, lo:hi])

    def ffn_chunk(r, h, c, pending):
        gate, up = pending
        pending = gate_up(h, c + 1) if c + 1 < n_chunks else None
        f = (jax.nn.silu(gate) * up).astype(jnp.bfloat16)
        part = _dot(f, wd_ref[c * FF_CHUNK:(c + 1) * FF_CHUNK, :])
        if c == 0:
            acc_ref[r] = part
        else:
            acc_ref[r] += part
        return pending

    def out_stage(r, x1):
        o_ref[r * half:(r + 1) * half, :] = x1 + _rms(acc_ref[r], gffn_ref[...])

    x1_a, h_a = mix_stage(0)
    x1_b, h_b = mix_stage(1)
    pending = gate_up(h_a, 0)
    for c in range(n_chunks):
        pending = ffn_chunk(0, h_a, c, pending)
    pending = gate_up(h_b, 0)
    for c in range(n_chunks):
        pending = ffn_chunk(1, h_b, c, pending)
        if c == 0:
            out_stage(0, x1_a)
    out_stage(1, x1_b)


def _out_ffn(x2, ys2, ya2, g_attn, wo_s, wo_a, g_post, g_pre, wg, wu, wd, g_ffn):
    n_rows, d_model = x2.shape
    d_half = ys2.shape[1]
    d_ff = wg.shape[1]
    assert n_rows % ROW_TILE == 0 and d_ff % FF_CHUNK == 0
    row = lambda i: (i, 0)
    resident = lambda a: pl.BlockSpec(a.shape, lambda i: (0,) * a.ndim, pipeline_mode=pl.Buffered(1))
    consts = (g_attn, wo_s, wo_a, g_post, g_pre, wg, wu, wd, g_ffn)
    return pl.pallas_call(
        _out_ffn_kernel,
        out_shape=jax.ShapeDtypeStruct((n_rows, d_model), jnp.float32),
        grid=(n_rows // ROW_TILE,),
        in_specs=[
            pl.BlockSpec((ROW_TILE, d_model), row),
            pl.BlockSpec((ROW_TILE, d_half), row),
            pl.BlockSpec((ROW_TILE, d_half), row),
        ] + [resident(a) for a in consts],
        out_specs=pl.BlockSpec((ROW_TILE, d_model), row),
        scratch_shapes=[pltpu.VMEM((2, ROW_TILE // 2, d_model), jnp.float32)],
        compiler_params=pltpu.CompilerParams(
            dimension_semantics=("arbitrary",), vmem_limit_bytes=VMEM_LIMIT_BYTES),
        name="out_ffn",
    )(x2, ys2, ya2, *consts)


def _s5_params(a_re, a_im, log_dt, b_re, b_im, c_re, c_im, n_batch):
    f32 = jnp.float32
    a_re, a_im = a_re.astype(f32), a_im.astype(f32)
    b_re, b_im = b_re.astype(f32), b_im.astype(f32)
    n_groups = a_re.shape[0]
    n_clusters = n_groups // GROUPS_PER_CLUSTER
    dt = jnp.exp(log_dt.astype(f32))[:, None]
    mag = jnp.exp(a_re * dt)
    ang = a_im * dt
    lb_re = mag * jnp.cos(ang)
    lb_im = mag * jnp.sin(ang)
    nr = lb_re - 1.0
    den = a_re * a_re + a_im * a_im
    cr = (nr * a_re + lb_im * a_im) / den
    ci = (lb_im * a_re - nr * a_im) / den
    bb_re = cr[..., None] * b_re - ci[..., None] * b_im
    bb_im = cr[..., None] * b_im + ci[..., None] * b_re
    eye = jnp.eye(GROUPS_PER_CLUSTER, dtype=f32)

    def in_proj(bb):
        bb = bb.reshape(n_clusters, GROUPS_PER_CLUSTER, SSM_STATE, SSM_GROUP)
        m = jnp.einsum('cgph,gk->cghkp', bb, eye)
        return m.reshape(n_clusters, LANES, CLUSTER_STATE)

    def out_proj(cc):
        cc = cc.reshape(n_clusters, GROUPS_PER_CLUSTER, SSM_GROUP, SSM_STATE)
        m = jnp.einsum('cghp,gk->cgpkh', cc, eye)
        return m.reshape(n_clusters, CLUSTER_STATE, LANES)

    bbar = jnp.concatenate([in_proj(bb_re), in_proj(bb_im)], axis=2).astype(jnp.bfloat16)
    ctil = jnp.concatenate([out_proj(c_re.astype(f32)), -out_proj(c_im.astype(f32))],
                           axis=1).astype(jnp.bfloat16)
    lam_re = jnp.broadcast_to(lb_re.reshape(1, -1), (n_batch, lb_re.size))
    lam_im = jnp.broadcast_to(lb_im.reshape(1, -1), (n_batch, lb_im.size))
    return lam_re, lam_im, bbar, ctil


def _time_batch_permutation(n_batch):
    n = n_batch * PERM_T
    r = jnp.arange(n)
    src = (r % n_batch) * PERM_T + r // n_batch
    return (src[:, None] == jnp.arange(n)[None, :]).astype(jnp.bfloat16)


def _layer(x, g_pre_mix, w_in, ssm_a_re, ssm_a_im, ssm_log_dt, ssm_b_re, ssm_b_im, ssm_c_re,
           ssm_c_im, ssm_d, w_glu, b_glu, g_ssm_out, g_attn_out, w_out, g_post_mix, g_pre_ffn,
           w_gate, w_up, w_down, g_post_ffn):
    n_batch, seq, d_model = x.shape
    d_ssm = w_glu.shape[0]
    bf16 = jnp.bfloat16
    row = lambda g: g.reshape(1, -1).astype(jnp.float32)
    lam_re, lam_im, bbar, ctil = _s5_params(ssm_a_re, ssm_a_im, ssm_log_dt, ssm_b_re, ssm_b_im,
                                            ssm_c_re, ssm_c_im, n_batch)
    q, k, v, y_ssm = _mixer_in(
        x, row(g_pre_mix), w_in.astype(bf16), _time_batch_permutation(n_batch), lam_re, lam_im,
        bbar, ctil, row(ssm_d), w_glu.astype(bf16), row(b_glu), row(g_ssm_out))
    y_attn = _moba(q, k, v)
    n_rows = n_batch * seq
    out = _out_ffn(
        x.reshape(n_rows, d_model), y_ssm.reshape(n_rows, d_ssm), y_attn.reshape(n_rows, -1),
        row(g_attn_out), w_out[:d_ssm].astype(bf16), w_out[d_ssm:].astype(bf16), row(g_post_mix),
        row(g_pre_ffn), w_gate.astype(bf16), w_up.astype(bf16), w_down.astype(bf16),
        row(g_post_ffn))
    return out.reshape(n_batch, seq, d_model)


def kernel(x, g_pre_mix, w_in, ssm_a_re, ssm_a_im, ssm_log_dt, ssm_b_re, ssm_b_im, ssm_c_re, ssm_c_im, ssm_d, w_glu, b_glu, g_ssm_out, g_attn_out, w_out, g_post_mix, g_pre_ffn, w_gate, w_up, w_down, g_post_ffn):
    for l in range(w_in.shape[0]):
        x = _layer(x, g_pre_mix[l], w_in[l], ssm_a_re[l], ssm_a_im[l], ssm_log_dt[l], ssm_b_re[l],
                   ssm_b_im[l], ssm_c_re[l], ssm_c_im[l], ssm_d[l], w_glu[l], b_glu[l],
                   g_ssm_out[l], g_attn_out[l], w_out[l], g_post_mix[l], g_pre_ffn[l], w_gate[l],
                   w_up[l], w_down[l], g_post_ffn[l])
    return x
```

```python
import collections
import functools
import math

import jax
import jax.numpy as jnp
from jax import lax
from jax.experimental import pallas as pl
from jax.experimental.pallas import tpu as pltpu

SSM_GROUP = 16
SSM_STATE = 64
HEAD_DIM = 64
MOBA_BLOCK = 256
MOBA_TOPK = 3
RMS_EPS = 1e-6

LANES = 128
BF16_SUBLANES = 16
VMEM_LIMIT_BYTES = 56 * 1024 * 1024

GROUPS_PER_CLUSTER = LANES // SSM_GROUP
CLUSTER_STATE = GROUPS_PER_CLUSTER * SSM_STATE
TIME_TILE = 64
PERM_T = BF16_SUBLANES
MOBA_SLABS = 2
ROW_TILE = 1024
FF_CHUNK = 256
MASK_NEG = -1e30
LOG2_E = math.log2(math.e)

_NT = (((1,), (1,)), ((), ()))
_SlabStages = collections.namedtuple('_SlabStages', 'gate_stage rank_stage score_stage value_stage')


def _rms(x, g):
    return x * lax.rsqrt(jnp.mean(x * x, axis=-1, keepdims=True) + RMS_EPS) * g


def _dot(a, b):
    return jnp.dot(a, b, preferred_element_type=jnp.float32)


def _dot_nt(a, b):
    return lax.dot_general(a, b, _NT, preferred_element_type=jnp.float32)


def _mixer_in_kernel(x_ref, gpre_ref, win_ref, perm_ref, permt_ref, lamre_ref, lamim_ref, wv_ref,
                     cz_ref, dmat_ref, dskip_ref, wglu_ref, bglu_ref, gssm_ref,
                     q_ref, k_ref, v_ref, yssm_ref,
                     state_ref, utm_ref, bu_ref, s_ref, y_ref, *, n_batch, n_clusters):
    lt = x_ref.shape[1]
    d_model = x_ref.shape[2]
    d_ssm = utm_ref.shape[2]
    rows = n_batch * lt
    half = rows // 2
    perm_rows = n_batch * PERM_T

    @pl.when(pl.program_id(0) == 0)
    def _():
        state_ref[...] = jnp.zeros_like(state_ref)

    x = x_ref[...].reshape(rows, d_model)
    h = _rms(x, gpre_ref[...]).astype(jnp.bfloat16)
    u = _dot(h, win_ref[:, 0:d_ssm]).astype(jnp.bfloat16)

    proj_cols = 2 * LANES
    projections = [(ref, col, tile) for ref, col in ((q_ref, 1), (k_ref, 2), (v_ref, 3))
                   for tile in range(d_ssm // proj_cols)]

    def project(count):
        for _ in range(min(count, len(projections))):
            ref, col, tile = projections.pop(0)
            c0 = col * d_ssm + tile * proj_cols
            z = _dot(h, win_ref[:, c0:c0 + proj_cols]).astype(jnp.bfloat16)
            ref[:, :, tile * proj_cols:(tile + 1) * proj_cols] = z.reshape(n_batch, lt, proj_cols)

    project(1)
    perm = perm_ref[...]
    pair_rows = perm_rows // 2
    for th in range(lt // PERM_T):
        chunk = jnp.concatenate(
            [u[b * lt + th * PERM_T: b * lt + (th + 1) * PERM_T] for b in range(n_batch)], axis=0)
        sorted_rows = _dot(perm, chunk).astype(jnp.bfloat16)
        for parity in range(2):
            utm_ref[parity, th * pair_rows:(th + 1) * pair_rows, :] = (
                sorted_rows[parity * pair_rows:(parity + 1) * pair_rows])
    for c in range(n_clusters):
        cols = slice(c * LANES, (c + 1) * LANES)
        bu_ref[c] = _dot(jnp.concatenate([utm_ref[0, :, cols], utm_ref[1, :, cols]], axis=1),
                         wv_ref[c])
        project(1)

    for c in range(n_clusters):
        lo, hi = c * CLUSTER_STATE, (c + 1) * CLUSTER_STATE
        cols = slice(c * LANES, (c + 1) * LANES)
        lam_re = lamre_ref[:, lo:hi]
        lam_im = lamim_ref[:, lo:hi]
        h_re = state_ref[0, :, lo:hi]
        h_im = state_ref[1, :, lo:hi]
        s_ref[c, 0:n_batch, 0:CLUSTER_STATE] = h_re.astype(jnp.bfloat16)
        s_ref[c, 0:n_batch, CLUSTER_STATE:2 * CLUSTER_STATE] = h_im.astype(jnp.bfloat16)
        for j in range(lt // 2):
            r0 = j * n_batch
            b_re = bu_ref[c, r0:r0 + n_batch, 0:CLUSTER_STATE]
            b_im = bu_ref[c, r0:r0 + n_batch, CLUSTER_STATE:2 * CLUSTER_STATE]
            n_re = lam_re * h_re - lam_im * h_im + b_re
            n_im = lam_re * h_im + lam_im * h_re + b_im
            h_re, h_im = n_re, n_im
            r1 = r0 + n_batch
            s_ref[c, r1:r1 + n_batch, 0:CLUSTER_STATE] = n_re.astype(jnp.bfloat16)
            s_ref[c, r1:r1 + n_batch, CLUSTER_STATE:2 * CLUSTER_STATE] = n_im.astype(jnp.bfloat16)
        state_ref[0, :, lo:hi] = h_re
        state_ref[1, :, lo:hi] = h_im
        z = _dot(s_ref[c], cz_ref[c])
        y_ref[0, :, cols] = z[:half, LANES:] + _dot(utm_ref[0, :, cols], dmat_ref[c])
        y_ref[1, :, cols] = z[n_batch:, :LANES]

    y = y_ref[...] + dskip_ref[...] * utm_ref[...].astype(jnp.float32)
    y = jax.nn.gelu(y.reshape(rows, d_ssm))
    glu = _dot(y.astype(jnp.bfloat16), wglu_ref[...])
    project(len(projections))
    y = y * jax.nn.sigmoid(glu + bglu_ref[...])
    yn = _rms(y, gssm_ref[...]).astype(jnp.bfloat16)
    perm_t = permt_ref[...]
    for th in range(lt // PERM_T):
        sorted_rows = jnp.concatenate(
            [yn[parity * half + th * pair_rows: parity * half + (th + 1) * pair_rows]
             for parity in range(2)], axis=0)
        back = _dot(perm_t, sorted_rows).astype(jnp.bfloat16)
        for b in range(n_batch):
            yssm_ref[b, th * PERM_T:(th + 1) * PERM_T, :] = back[b * PERM_T:(b + 1) * PERM_T]


def _mixer_in(x, g_pre, w_in, perm, perm_t, lam_re, lam_im, wv, cz, dmat, d_skip, w_glu, b_glu,
              g_ssm):
    n_batch, seq, d_model = x.shape
    d_ssm = w_glu.shape[0]
    n_clusters = d_ssm // LANES
    lt = TIME_TILE
    rows = n_batch * lt
    assert seq % lt == 0 and lt % PERM_T == 0 and n_batch % BF16_SUBLANES == 0
    resident = lambda a: pl.BlockSpec(a.shape, lambda i: (0,) * a.ndim, pipeline_mode=pl.Buffered(1))
    act_spec = pl.BlockSpec((n_batch, lt, d_ssm), lambda i: (0, i, 0))
    act_shape = jax.ShapeDtypeStruct((n_batch, seq, d_ssm), jnp.bfloat16)
    consts = (g_pre, w_in, perm, perm_t, lam_re, lam_im, wv, cz, dmat, d_skip, w_glu, b_glu, g_ssm)
    return pl.pallas_call(
        functools.partial(_mixer_in_kernel, n_batch=n_batch, n_clusters=n_clusters),
        out_shape=(act_shape,) * 4,
        grid=(seq // lt,),
        in_specs=[pl.BlockSpec((n_batch, lt, d_model), lambda i: (0, i, 0))]
        + [resident(a) for a in consts],
        out_specs=(act_spec,) * 4,
        scratch_shapes=[
            pltpu.VMEM((2, n_batch, n_clusters * CLUSTER_STATE), jnp.float32),
            pltpu.VMEM((2, rows // 2, d_ssm), jnp.bfloat16),
            pltpu.VMEM((n_clusters, rows // 2, 2 * CLUSTER_STATE), jnp.float32),
            pltpu.VMEM((n_clusters, rows // 2 + n_batch, 2 * CLUSTER_STATE), jnp.bfloat16),
            pltpu.VMEM((2, rows // 2, d_ssm), jnp.float32),
        ],
        compiler_params=pltpu.CompilerParams(
            dimension_semantics=("arbitrary",), vmem_limit_bytes=VMEM_LIMIT_BYTES),
        name="mixer_in",
    )(x, *consts)


def _block_order(i):
    return [i] + list(range(i))


def _score_ops(q_aug_t, ka_ref, i, causal, out):
    for j in _block_order(i):
        s = _dot(ka_ref[j * MOBA_BLOCK:(j + 1) * MOBA_BLOCK, :], q_aug_t)
        out.append(jnp.where(causal, s, MASK_NEG) if j == i else s)
        yield


def _value_ops(scores, vt_ref, i, out):
    m = functools.reduce(jnp.maximum, [jnp.max(s, axis=0, keepdims=True) for s in scores])
    acc = None
    for j, s in zip(_block_order(i), scores):
        part = _dot(vt_ref[:, j * MOBA_BLOCK:(j + 1) * MOBA_BLOCK],
                    jnp.exp2(s - m).astype(jnp.bfloat16))
        acc = part if acc is None else acc + part
        yield
    out.append(acc[:HEAD_DIM] / acc[HEAD_DIM:HEAD_DIM + 1])


def _interleave(*generators):
    live = list(generators)
    while live:
        for g in list(live):
            try:
                next(g)
            except StopIteration:
                live.remove(g)


def _moba_kernel(q_ref, k_ref, v_ref, o_ref, ka_ref, vt_ref, *, n_blocks):
    stages = []
    for s in range(q_ref.shape[1] // LANES):
        cols = slice(s * LANES, (s + 1) * LANES)
        stages.append(_moba_slab(q_ref.at[:, cols], k_ref.at[:, cols], v_ref.at[:, cols],
                                 o_ref.at[:, cols], ka_ref.at[s], vt_ref.at[s], n_blocks))

    items = [(s, i) for s in range(len(stages)) for i in range(n_blocks - 1, -1, -1)]
    gates, ranks, scores = {}, {}, {}
    for step in range(len(items) + 3):
        if step < len(items):
            s, i = items[step]
            gates[s, i] = stages[s].gate_stage(i)
        if 0 <= step - 1 < len(items):
            s, i = items[step - 1]
            ranks[s, i] = stages[s].rank_stage(i, gates.pop((s, i)))
        pending = []
        if 0 <= step - 2 < len(items):
            s, i = items[step - 2]
            scores[s, i] = ([], [])
            pending.append(stages[s].score_stage(i, ranks.pop((s, i)), scores[s, i]))
        if 0 <= step - 3 < len(items):
            s, i = items[step - 3]
            pending.append(stages[s].value_stage(i, scores.pop((s, i))))
        _interleave(*pending)


def _moba_slab(q_ref, k_ref, v_ref, o_ref, ka_ref, vt_ref, n_blocks):
    seq = q_ref.shape[0]
    n_pairs = n_blocks * n_blocks
    assert LANES == 2 * HEAD_DIM and n_pairs <= HEAD_DIM
    lane = lax.broadcasted_iota(jnp.int32, (1, LANES), 1)
    own0 = lane < HEAD_DIM
    causal = (lax.broadcasted_iota(jnp.int32, (MOBA_BLOCK, MOBA_BLOCK), 0)
              <= lax.broadcasted_iota(jnp.int32, (MOBA_BLOCK, MOBA_BLOCK), 1))

    key_blk = lax.broadcasted_iota(jnp.int32, (seq, 1), 0) // MOBA_BLOCK
    lane_full = lax.broadcasted_iota(jnp.int32, (seq, LANES), 1)
    k_all = k_ref[...]
    ka_ref[0] = jnp.where(own0, k_all,
                          jnp.where(lane_full == key_blk + HEAD_DIM, 1.0, 0.0).astype(jnp.bfloat16))
    ka_ref[1] = jnp.where(own0, jnp.where(lane_full == key_blk, 1.0, 0.0).astype(jnp.bfloat16), k_all)
    v_t = v_ref[...].T
    ones_rows = jnp.where(lax.broadcasted_iota(jnp.int32, (BF16_SUBLANES, seq), 0) == 0, 1.0, 0.0)
    ones_rows = ones_rows.astype(jnp.bfloat16)
    vt_ref[0] = jnp.concatenate([v_t[:HEAD_DIM], ones_rows], axis=0)
    vt_ref[1] = jnp.concatenate([v_t[HEAD_DIM:], ones_rows], axis=0)

    k_mean = jnp.sum(k_all.astype(jnp.float32).reshape(n_blocks, MOBA_BLOCK, LANES), axis=1)
    k_mean = k_mean * (1.0 / MOBA_BLOCK)
    km_a = jnp.broadcast_to(k_mean[:, None, :], (n_blocks, n_blocks, LANES)).reshape(n_pairs, LANES)
    km_b = jnp.broadcast_to(k_mean[None, :, :], (n_blocks, n_blocks, LANES)).reshape(n_pairs, LANES)
    pad = [jnp.zeros((HEAD_DIM - n_pairs, LANES), jnp.float32)] if n_pairs < HEAD_DIM else []

    def both_heads(km):
        return jnp.concatenate(
            [jnp.where(own0, km, 0.0)] + pad + [jnp.where(own0, 0.0, km)] + pad, axis=0)

    def split(m):
        hi = m.astype(jnp.bfloat16)
        return [hi, (m - hi.astype(jnp.float32)).astype(jnp.bfloat16)]
    gate_rhs = jnp.concatenate(split(both_heads(km_a)) + split(both_heads(km_b)), axis=0)

    trip = lane % HEAD_DIM
    trip_j, trip_jp = trip // n_blocks, trip % n_blocks
    trip_ok = trip < n_pairs
    r_idx = lax.broadcasted_iota(jnp.int32, (LANES, LANES), 0)
    c_idx = lax.broadcasted_iota(jnp.int32, (LANES, LANES), 1)
    r_bias_lane = jnp.where(r_idx < HEAD_DIM, HEAD_DIM, 0) + (r_idx % HEAD_DIM) // n_blocks
    rank_sum = jnp.where((c_idx == r_bias_lane) & (r_idx % HEAD_DIM < n_pairs), 1.0, 0.0)
    rank_sum = rank_sum.astype(jnp.bfloat16)
    bias_blk = jnp.where(own0, lane, lane - HEAD_DIM)
    bias_lane = bias_blk < n_blocks

    def gate_stage(i):
        q_i = q_ref[i * MOBA_BLOCK:(i + 1) * MOBA_BLOCK, :]
        return _dot_nt(q_i * jnp.bfloat16(HEAD_DIM ** -0.5), gate_rhs)

    def rank_stage(i, gates):
        gate_j = gates[:, 0:LANES] + gates[:, LANES:2 * LANES]
        gate_jp = gates[:, 2 * LANES:3 * LANES] + gates[:, 3 * LANES:]
        beats = (((gate_jp > gate_j) | ((gate_jp == gate_j) & (trip_jp < trip_j)))
                 & (trip_jp < i) & trip_ok)
        return _dot(jnp.where(beats, 1.0, 0.0).astype(jnp.bfloat16), rank_sum)

    def score_stage(i, rank, out):
        keep = ((rank < MOBA_TOPK) & (bias_blk < i)) | (bias_blk == i) | ~bias_lane
        bias = jnp.where(keep, 0.0, MASK_NEG).astype(jnp.bfloat16)
        q_i = q_ref[i * MOBA_BLOCK:(i + 1) * MOBA_BLOCK, :]
        q_s = (q_i.astype(jnp.float32) * (HEAD_DIM ** -0.5 * LOG2_E)).astype(jnp.bfloat16)
        yield from _score_ops(jnp.where(own0, q_s, bias).T, ka_ref.at[0], i, causal, out[0])
        yield from _score_ops(jnp.where(own0, bias, q_s).T, ka_ref.at[1], i, causal, out[1])

    def value_stage(i, scores):
        out = []
        yield from _value_ops(scores[0], vt_ref.at[0], i, out)
        yield from _value_ops(scores[1], vt_ref.at[1], i, out)
        o_ref[i * MOBA_BLOCK:(i + 1) * MOBA_BLOCK, :] = (
            jnp.concatenate(out, axis=0).T.astype(o_ref.dtype))

    return _SlabStages(gate_stage, rank_stage, score_stage, value_stage)


def _moba(q, k, v):
    n_batch, seq, d_attn = q.shape
    assert seq % MOBA_BLOCK == 0 and d_attn % LANES == 0
    n_blocks = seq // MOBA_BLOCK
    width = MOBA_SLABS * LANES
    assert d_attn % width == 0
    spec = pl.BlockSpec((None, seq, width), lambda b, s: (b, 0, s))
    return pl.pallas_call(
        functools.partial(_moba_kernel, n_blocks=n_blocks),
        out_shape=jax.ShapeDtypeStruct((n_batch, seq, d_attn), jnp.bfloat16),
        grid=(n_batch, d_attn // width),
        in_specs=[spec, spec, spec],
        out_specs=spec,
        scratch_shapes=[
            pltpu.VMEM((MOBA_SLABS, LANES // HEAD_DIM, seq, LANES), jnp.bfloat16),
            pltpu.VMEM((MOBA_SLABS, LANES // HEAD_DIM, HEAD_DIM + BF16_SUBLANES, seq),
                       jnp.bfloat16)],
        compiler_params=pltpu.CompilerParams(
            dimension_semantics=("arbitrary", "arbitrary"), vmem_limit_bytes=VMEM_LIMIT_BYTES),
        name="moba",
    )(q, k, v)


def _out_ffn_kernel(x_ref, ys_ref, ya_ref, gattn_ref, wo_ref, gpost_ref, gpre_ref,
                    wg_ref, wu_ref, wd_ref, gffn_ref, o_ref, acc_ref):
    n_chunks = wg_ref.shape[1] // FF_CHUNK
    half = x_ref.shape[0] // 2
    d_ssm = ys_ref.shape[1]

    def mix_stage(r):
        rows = slice(r * half, (r + 1) * half)
        ya = _rms(ya_ref[rows, :].astype(jnp.float32), gattn_ref[...]).astype(jnp.bfloat16)
        mixed = _dot(ys_ref[rows, :], wo_ref[:d_ssm, :]) + _dot(ya, wo_ref[d_ssm:, :])
        x1 = x_ref[rows, :] + _rms(mixed, gpost_ref[...])
        return x1, _rms(x1, gpre_ref[...]).astype(jnp.bfloat16)

    def gate_up(h, c):
        lo, hi = c * FF_CHUNK, (c + 1) * FF_CHUNK
        return _dot(h, wg_ref[:, lo:hi]), _dot(h, wu_ref[:, lo:hi])

    def ffn_chunk(r, h, c, pending):
        gate, up = pending
        pending = gate_up(h, c + 1) if c + 1 < n_chunks else None
        f = (jax.nn.silu(gate) * up).astype(jnp.bfloat16)
        part = _dot(f, wd_ref[c * FF_CHUNK:(c + 1) * FF_CHUNK, :])
        if c == 0:
            acc_ref[r] = part
        else:
            acc_ref[r] += part
        return pending

    def out_stage(r, x1):
        o_ref[r * half:(r + 1) * half, :] = x1 + _rms(acc_ref[r], gffn_ref[...])

    x1_a, h_a = mix_stage(0)
    x1_b, h_b = mix_stage(1)
    pending = gate_up(h_a, 0)
    for c in range(n_chunks):
        pending = ffn_chunk(0, h_a, c, pending)
    pending = gate_up(h_b, 0)
    for c in range(n_chunks):
        pending = ffn_chunk(1, h_b, c, pending)
        if c == 0:
            out_stage(0, x1_a)
    out_stage(1, x1_b)


def _out_ffn(x2, ys2, ya2, g_attn, wo, g_post, g_pre, wg, wu, wd, g_ffn):
    n_rows, d_model = x2.shape
    d_half = ys2.shape[1]
    d_ff = wg.shape[1]
    assert n_rows % ROW_TILE == 0 and d_ff % FF_CHUNK == 0
    row = lambda i: (i, 0)
    resident = lambda a: pl.BlockSpec(a.shape, lambda i: (0,) * a.ndim, pipeline_mode=pl.Buffered(1))
    consts = (g_attn, wo, g_post, g_pre, wg, wu, wd, g_ffn)
    return pl.pallas_call(
        _out_ffn_kernel,
        out_shape=jax.ShapeDtypeStruct((n_rows, d_model), jnp.float32),
        grid=(n_rows // ROW_TILE,),
        in_specs=[
            pl.BlockSpec((ROW_TILE, d_model), row),
            pl.BlockSpec((ROW_TILE, d_half), row),
            pl.BlockSpec((ROW_TILE, d_half), row),
        ] + [resident(a) for a in consts],
        out_specs=pl.BlockSpec((ROW_TILE, d_model), row),
        scratch_shapes=[pltpu.VMEM((2, ROW_TILE // 2, d_model), jnp.float32)],
        compiler_params=pltpu.CompilerParams(
            dimension_semantics=("arbitrary",), vmem_limit_bytes=VMEM_LIMIT_BYTES),
        name="out_ffn",
    )(x2, ys2, ya2, *consts)


def _s5_params(a_re, a_im, log_dt, b_re, b_im, c_re, c_im, n_batch):
    f32 = jnp.float32
    a_re, a_im = a_re.astype(f32), a_im.astype(f32)
    b_re, b_im = b_re.astype(f32), b_im.astype(f32)
    n_groups = a_re.shape[0]
    n_clusters = n_groups // GROUPS_PER_CLUSTER
    dt = jnp.exp(log_dt.astype(f32))[:, None]
    mag = jnp.exp(a_re * dt)
    ang = a_im * dt
    lb_re = mag * jnp.cos(ang)
    lb_im = mag * jnp.sin(ang)
    nr = lb_re - 1.0
    den = a_re * a_re + a_im * a_im
    cr = (nr * a_re + lb_im * a_im) / den
    ci = (lb_im * a_re - nr * a_im) / den
    bb_re = cr[..., None] * b_re - ci[..., None] * b_im
    bb_im = cr[..., None] * b_im + ci[..., None] * b_re
    eye = jnp.eye(GROUPS_PER_CLUSTER, dtype=f32)

    def in_proj(bb):
        bb = bb.reshape(n_clusters, GROUPS_PER_CLUSTER, SSM_STATE, SSM_GROUP)
        m = jnp.einsum('cgph,gk->cghkp', bb, eye)
        return m.reshape(n_clusters, LANES, CLUSTER_STATE)

    def out_proj(cc):
        cc = cc.reshape(n_clusters, GROUPS_PER_CLUSTER, SSM_GROUP, SSM_STATE)
        m = jnp.einsum('cghp,gk->cgpkh', cc, eye)
        return m.reshape(n_clusters, CLUSTER_STATE, LANES)

    def rotate(m_re, m_im, l_re, l_im):
        return m_re * l_re - m_im * l_im, m_re * l_im + m_im * l_re

    def state_cols(m_re, m_im):
        return jnp.concatenate([in_proj(m_re), in_proj(m_im)], axis=2)

    def state_rows(m_re, m_im):
        return jnp.concatenate([out_proj(m_re), -out_proj(m_im)], axis=1)

    bl_re, bl_im = rotate(bb_re, bb_im, lb_re[..., None], lb_im[..., None])
    wv = jnp.concatenate([state_cols(bl_re, bl_im), state_cols(bb_re, bb_im)], axis=1)
    c_re, c_im = c_re.astype(f32), c_im.astype(f32)
    cl_re, cl_im = rotate(c_re, c_im, lb_re[:, None, :], lb_im[:, None, :])
    cz = jnp.concatenate([state_rows(c_re, c_im), state_rows(cl_re, cl_im)], axis=2)
    direct = (jnp.einsum('gop,gpi->gio', c_re, bb_re) - jnp.einsum('gop,gpi->gio', c_im, bb_im))
    direct = direct.reshape(n_clusters, GROUPS_PER_CLUSTER, SSM_GROUP, SSM_GROUP)
    dmat = jnp.einsum('cgio,gk->cgiko', direct, eye).reshape(n_clusters, LANES, LANES)
    l2_re, l2_im = rotate(lb_re, lb_im, lb_re, lb_im)
    lam2_re = jnp.broadcast_to(l2_re.reshape(1, -1), (n_batch, l2_re.size))
    lam2_im = jnp.broadcast_to(l2_im.reshape(1, -1), (n_batch, l2_im.size))
    bf16 = jnp.bfloat16
    return lam2_re, lam2_im, wv.astype(bf16), cz.astype(bf16), dmat.astype(bf16)


def _time_batch_permutation(n_batch):
    n = n_batch * PERM_T
    r = jnp.arange(n)
    parity, rest = r // (n // 2), r % (n // 2)
    src = (rest % n_batch) * PERM_T + 2 * (rest // n_batch) + parity
    return (src[:, None] == jnp.arange(n)[None, :]).astype(jnp.bfloat16)


def _layer(x, g_pre_mix, w_in, ssm_a_re, ssm_a_im, ssm_log_dt, ssm_b_re, ssm_b_im, ssm_c_re,
           ssm_c_im, ssm_d, w_glu, b_glu, g_ssm_out, g_attn_out, w_out, g_post_mix, g_pre_ffn,
           w_gate, w_up, w_down, g_post_ffn):
    n_batch, seq, d_model = x.shape
    d_ssm = w_glu.shape[0]
    bf16 = jnp.bfloat16
    row = lambda g: g.reshape(1, -1).astype(jnp.float32)
    lam_re, lam_im, wv, cz, dmat = _s5_params(ssm_a_re, ssm_a_im, ssm_log_dt, ssm_b_re, ssm_b_im,
                                              ssm_c_re, ssm_c_im, n_batch)
    perm = _time_batch_permutation(n_batch)
    q, k, v, y_ssm = _mixer_in(
        x, row(g_pre_mix), w_in, perm, perm.T, lam_re, lam_im, wv, cz, dmat, row(ssm_d), w_glu,
        row(b_glu), row(g_ssm_out))
    y_attn = _moba(q, k, v)
    n_rows = n_batch * seq
    out = _out_ffn(
        x.reshape(n_rows, d_model), y_ssm.reshape(n_rows, d_ssm), y_attn.reshape(n_rows, -1),
        row(g_attn_out), w_out.astype(bf16), row(g_post_mix), row(g_pre_ffn), w_gate.astype(bf16),
        w_up.astype(bf16), w_down.astype(bf16), row(g_post_ffn))
    return out.reshape(n_batch, seq, d_model)


def kernel(x, g_pre_mix, w_in, ssm_a_re, ssm_a_im, ssm_log_dt, ssm_b_re, ssm_b_im, ssm_c_re, ssm_c_im, ssm_d, w_glu, b_glu, g_ssm_out, g_attn_out, w_out, g_post_mix, g_pre_ffn, w_gate, w_up, w_down, g_post_ffn):
    for l in range(w_in.shape[0]):
        x = _layer(x, g_pre_mix[l], w_in[l], ssm_a_re[l], ssm_a_im[l], ssm_log_dt[l], ssm_b_re[l],
                   ssm_b_im[l], ssm_c_re[l], ssm_c_im[l], ssm_d[l], w_glu[l], b_glu[l],
                   g_ssm_out[l], g_attn_out[l], w_out[l], g_post_mix[l], g_pre_ffn[l], w_gate[l],
                   w_up[l], w_down[l], g_post_ffn[l])
    return x
```

```python
import collections
import functools
import math

import jax
import jax.numpy as jnp
from jax import lax
from jax.experimental import pallas as pl
from jax.experimental.pallas import tpu as pltpu

SSM_GROUP = 16
SSM_STATE = 64
HEAD_DIM = 64
MOBA_BLOCK = 256
MOBA_TOPK = 3
RMS_EPS = 1e-6

LANES = 128
BF16_SUBLANES = 16
VMEM_LIMIT_BYTES = 56 * 1024 * 1024

GROUPS_PER_CLUSTER = LANES // SSM_GROUP
CLUSTER_STATE = GROUPS_PER_CLUSTER * SSM_STATE
TIME_TILE = 64
PERM_T = BF16_SUBLANES
MOBA_SLABS = 4
ROW_TILE = 1024
FF_CHUNK = 256
MASK_NEG = -1e30
LOG2_E = math.log2(math.e)

_NT = (((1,), (1,)), ((), ()))
_SlabStages = collections.namedtuple('_SlabStages', 'gate_stage rank_stage score_stage value_stage')


def _rms(x, g):
    return x * lax.rsqrt(jnp.mean(x * x, axis=-1, keepdims=True) + RMS_EPS) * g


def _dot(a, b):
    return jnp.dot(a, b, preferred_element_type=jnp.float32)


def _dot_nt(a, b):
    return lax.dot_general(a, b, _NT, preferred_element_type=jnp.float32)


def _mixer_in_kernel(x_ref, gpre_ref, win_ref, perm_ref, permt_ref, lamre_ref, lamim_ref, wv_ref,
                     cz_ref, dmat_ref, dskip_ref, wglu_ref, bglu_ref, gssm_ref,
                     q_ref, k_ref, v_ref, yssm_ref,
                     state_ref, utm_ref, bu_ref, s_ref, y_ref, *, n_batch, n_clusters):
    lt = x_ref.shape[1]
    d_model = x_ref.shape[2]
    d_ssm = utm_ref.shape[2]
    rows = n_batch * lt
    half = rows // 2
    perm_rows = n_batch * PERM_T

    @pl.when(pl.program_id(0) == 0)
    def _():
        state_ref[...] = jnp.zeros_like(state_ref)

    x = x_ref[...].reshape(rows, d_model)
    h = _rms(x, gpre_ref[...]).astype(jnp.bfloat16)
    u = _dot(h, win_ref[:, 0:d_ssm]).astype(jnp.bfloat16)

    proj_cols = 2 * LANES
    projections = [(ref, col, tile) for ref, col in ((q_ref, 1), (k_ref, 2), (v_ref, 3))
                   for tile in range(d_ssm // proj_cols)]

    def project(count):
        for _ in range(min(count, len(projections))):
            ref, col, tile = projections.pop(0)
            c0 = col * d_ssm + tile * proj_cols
            z = _dot(h, win_ref[:, c0:c0 + proj_cols]).astype(jnp.bfloat16)
            ref[:, :, tile * proj_cols:(tile + 1) * proj_cols] = z.reshape(n_batch, lt, proj_cols)

    project(1)
    perm = perm_ref[...]
    pair_rows = perm_rows // 2
    for th in range(lt // PERM_T):
        chunk = jnp.concatenate(
            [u[b * lt + th * PERM_T: b * lt + (th + 1) * PERM_T] for b in range(n_batch)], axis=0)
        sorted_rows = _dot(perm, chunk).astype(jnp.bfloat16)
        for parity in range(2):
            utm_ref[parity, th * pair_rows:(th + 1) * pair_rows, :] = (
                sorted_rows[parity * pair_rows:(parity + 1) * pair_rows])
    for c in range(n_clusters):
        cols = slice(c * LANES, (c + 1) * LANES)
        bu_ref[c] = _dot(jnp.concatenate([utm_ref[0, :, cols], utm_ref[1, :, cols]], axis=1),
                         wv_ref[c])
        project(1)

    for c in range(n_clusters):
        lo, hi = c * CLUSTER_STATE, (c + 1) * CLUSTER_STATE
        cols = slice(c * LANES, (c + 1) * LANES)
        lam_re = lamre_ref[:, lo:hi]
        lam_im = lamim_ref[:, lo:hi]
        h_re = state_ref[0, :, lo:hi]
        h_im = state_ref[1, :, lo:hi]
        s_ref[c, 0:n_batch, 0:CLUSTER_STATE] = h_re.astype(jnp.bfloat16)
        s_ref[c, 0:n_batch, CLUSTER_STATE:2 * CLUSTER_STATE] = h_im.astype(jnp.bfloat16)
        for j in range(lt // 2):
            r0 = j * n_batch
            b_re = bu_ref[c, r0:r0 + n_batch, 0:CLUSTER_STATE]
            b_im = bu_ref[c, r0:r0 + n_batch, CLUSTER_STATE:2 * CLUSTER_STATE]
            n_re = lam_re * h_re - lam_im * h_im + b_re
            n_im = lam_re * h_im + lam_im * h_re + b_im
            h_re, h_im = n_re, n_im
            r1 = r0 + n_batch
            s_ref[c, r1:r1 + n_batch, 0:CLUSTER_STATE] = n_re.astype(jnp.bfloat16)
            s_ref[c, r1:r1 + n_batch, CLUSTER_STATE:2 * CLUSTER_STATE] = n_im.astype(jnp.bfloat16)
        state_ref[0, :, lo:hi] = h_re
        state_ref[1, :, lo:hi] = h_im
        z = _dot(s_ref[c], cz_ref[c])
        y_ref[0, :, cols] = z[:half, LANES:] + _dot(utm_ref[0, :, cols], dmat_ref[c])
        y_ref[1, :, cols] = z[n_batch:, :LANES]

    y = y_ref[...] + dskip_ref[...] * utm_ref[...].astype(jnp.float32)
    y = jax.nn.gelu(y.reshape(rows, d_ssm))
    glu = _dot(y.astype(jnp.bfloat16), wglu_ref[...])
    project(len(projections))
    y = y * jax.nn.sigmoid(glu + bglu_ref[...])
    yn = _rms(y, gssm_ref[...]).astype(jnp.bfloat16)
    perm_t = permt_ref[...]
    for th in range(lt // PERM_T):
        sorted_rows = jnp.concatenate(
            [yn[parity * half + th * pair_rows: parity * half + (th + 1) * pair_rows]
             for parity in range(2)], axis=0)
        back = _dot(perm_t, sorted_rows).astype(jnp.bfloat16)
        for b in range(n_batch):
            yssm_ref[b, th * PERM_T:(th + 1) * PERM_T, :] = back[b * PERM_T:(b + 1) * PERM_T]


def _mixer_in(x, g_pre, w_in, perm, perm_t, lam_re, lam_im, wv, cz, dmat, d_skip, w_glu, b_glu,
              g_ssm):
    n_batch, seq, d_model = x.shape
    d_ssm = w_glu.shape[0]
    n_clusters = d_ssm // LANES
    lt = TIME_TILE
    rows = n_batch * lt
    assert seq % lt == 0 and lt % PERM_T == 0 and n_batch % BF16_SUBLANES == 0
    resident = lambda a: pl.BlockSpec(a.shape, lambda i: (0,) * a.ndim, pipeline_mode=pl.Buffered(1))
    act_spec = pl.BlockSpec((n_batch, lt, d_ssm), lambda i: (0, i, 0))
    act_shape = jax.ShapeDtypeStruct((n_batch, seq, d_ssm), jnp.bfloat16)
    consts = (g_pre, w_in, perm, perm_t, lam_re, lam_im, wv, cz, dmat, d_skip, w_glu, b_glu, g_ssm)
    return pl.pallas_call(
        functools.partial(_mixer_in_kernel, n_batch=n_batch, n_clusters=n_clusters),
        out_shape=(act_shape,) * 4,
        grid=(seq // lt,),
        in_specs=[pl.BlockSpec((n_batch, lt, d_model), lambda i: (0, i, 0))]
        + [resident(a) for a in consts],
        out_specs=(act_spec,) * 4,
        scratch_shapes=[
            pltpu.VMEM((2, n_batch, n_clusters * CLUSTER_STATE), jnp.float32),
            pltpu.VMEM((2, rows // 2, d_ssm), jnp.bfloat16),
            pltpu.VMEM((n_clusters, rows // 2, 2 * CLUSTER_STATE), jnp.float32),
            pltpu.VMEM((n_clusters, rows // 2 + n_batch, 2 * CLUSTER_STATE), jnp.bfloat16),
            pltpu.VMEM((2, rows // 2, d_ssm), jnp.float32),
        ],
        compiler_params=pltpu.CompilerParams(
            dimension_semantics=("arbitrary",), vmem_limit_bytes=VMEM_LIMIT_BYTES),
        name="mixer_in",
    )(x, *consts)


def _block_order(i):
    return [i] + list(range(i))


def _score_ops(q_aug_t, ka_ref, i, causal, out):
    for j in _block_order(i):
        s = _dot(ka_ref[j * MOBA_BLOCK:(j + 1) * MOBA_BLOCK, :], q_aug_t)
        out.append(jnp.where(causal, s, MASK_NEG) if j == i else s)
        yield


def _value_ops(scores, vt_ref, i, out):
    m = functools.reduce(jnp.maximum, [jnp.max(s, axis=0, keepdims=True) for s in scores])
    acc = None
    for j, s in zip(_block_order(i), scores):
        part = _dot(vt_ref[:, j * MOBA_BLOCK:(j + 1) * MOBA_BLOCK],
                    jnp.exp2(s - m).astype(jnp.bfloat16))
        acc = part if acc is None else acc + part
        yield
    out.append(acc[:HEAD_DIM] / acc[HEAD_DIM:HEAD_DIM + 1])


def _interleave(*generators):
    live = list(generators)
    while live:
        for g in list(live):
            try:
                next(g)
            except StopIteration:
                live.remove(g)


def _moba_kernel(q_ref, k_ref, v_ref, o_ref, ka_ref, vt_ref, *, n_blocks):
    stages = []
    for s in range(q_ref.shape[1] // LANES):
        cols = slice(s * LANES, (s + 1) * LANES)
        stages.append(_moba_slab(q_ref.at[:, cols], k_ref.at[:, cols], v_ref.at[:, cols],
                                 o_ref.at[:, cols], ka_ref.at[s], vt_ref.at[s], n_blocks))

    items = [(s, i) for s in range(len(stages)) for i in range(n_blocks - 1, -1, -1)]
    gates, ranks, scores = {}, {}, {}
    for step in range(len(items) + 3):
        if step < len(items):
            s, i = items[step]
            gates[s, i] = stages[s].gate_stage(i)
        if 0 <= step - 1 < len(items):
            s, i = items[step - 1]
            ranks[s, i] = stages[s].rank_stage(i, gates.pop((s, i)))
        pending = []
        if 0 <= step - 2 < len(items):
            s, i = items[step - 2]
            scores[s, i] = ([], [])
            pending.append(stages[s].score_stage(i, ranks.pop((s, i)), scores[s, i]))
        if 0 <= step - 3 < len(items):
            s, i = items[step - 3]
            pending.append(stages[s].value_stage(i, scores.pop((s, i))))
        _interleave(*pending)


def _moba_slab(q_ref, k_ref, v_ref, o_ref, ka_ref, vt_ref, n_blocks):
    seq = q_ref.shape[0]
    n_pairs = n_blocks * n_blocks
    assert LANES == 2 * HEAD_DIM and n_pairs <= HEAD_DIM
    lane = lax.broadcasted_iota(jnp.int32, (1, LANES), 1)
    own0 = lane < HEAD_DIM
    causal = (lax.broadcasted_iota(jnp.int32, (MOBA_BLOCK, MOBA_BLOCK), 0)
              <= lax.broadcasted_iota(jnp.int32, (MOBA_BLOCK, MOBA_BLOCK), 1))

    key_blk = lax.broadcasted_iota(jnp.int32, (seq, 1), 0) // MOBA_BLOCK
    lane_full = lax.broadcasted_iota(jnp.int32, (seq, LANES), 1)
    k_all = k_ref[...]
    ka_ref[0] = jnp.where(own0, k_all,
                          jnp.where(lane_full == key_blk + HEAD_DIM, 1.0, 0.0).astype(jnp.bfloat16))
    ka_ref[1] = jnp.where(own0, jnp.where(lane_full == key_blk, 1.0, 0.0).astype(jnp.bfloat16), k_all)
    v_t = v_ref[...].T
    ones_rows = jnp.where(lax.broadcasted_iota(jnp.int32, (BF16_SUBLANES, seq), 0) == 0, 1.0, 0.0)
    ones_rows = ones_rows.astype(jnp.bfloat16)
    vt_ref[0] = jnp.concatenate([v_t[:HEAD_DIM], ones_rows], axis=0)
    vt_ref[1] = jnp.concatenate([v_t[HEAD_DIM:], ones_rows], axis=0)

    k_mean = jnp.sum(k_all.astype(jnp.float32).reshape(n_blocks, MOBA_BLOCK, LANES), axis=1)
    k_mean = k_mean * (1.0 / MOBA_BLOCK)
    km_a = jnp.broadcast_to(k_mean[:, None, :], (n_blocks, n_blocks, LANES)).reshape(n_pairs, LANES)
    km_b = jnp.broadcast_to(k_mean[None, :, :], (n_blocks, n_blocks, LANES)).reshape(n_pairs, LANES)
    pad = [jnp.zeros((HEAD_DIM - n_pairs, LANES), jnp.float32)] if n_pairs < HEAD_DIM else []

    def both_heads(km):
        return jnp.concatenate(
            [jnp.where(own0, km, 0.0)] + pad + [jnp.where(own0, 0.0, km)] + pad, axis=0)

    def split(m):
        hi = m.astype(jnp.bfloat16)
        return [hi, (m - hi.astype(jnp.float32)).astype(jnp.bfloat16)]
    gate_rhs = jnp.concatenate(split(both_heads(km_a)) + split(both_heads(km_b)), axis=0)

    trip = lane % HEAD_DIM
    trip_j, trip_jp = trip // n_blocks, trip % n_blocks
    trip_ok = trip < n_pairs
    r_idx = lax.broadcasted_iota(jnp.int32, (LANES, LANES), 0)
    c_idx = lax.broadcasted_iota(jnp.int32, (LANES, LANES), 1)
    r_bias_lane = jnp.where(r_idx < HEAD_DIM, HEAD_DIM, 0) + (r_idx % HEAD_DIM) // n_blocks
    rank_sum = jnp.where((c_idx == r_bias_lane) & (r_idx % HEAD_DIM < n_pairs), 1.0, 0.0)
    rank_sum = rank_sum.astype(jnp.bfloat16)
    bias_blk = jnp.where(own0, lane, lane - HEAD_DIM)
    bias_lane = bias_blk < n_blocks

    def gate_stage(i):
        q_i = q_ref[i * MOBA_BLOCK:(i + 1) * MOBA_BLOCK, :]
        return _dot_nt(q_i * jnp.bfloat16(HEAD_DIM ** -0.5), gate_rhs)

    def rank_stage(i, gates):
        gate_j = gates[:, 0:LANES] + gates[:, LANES:2 * LANES]
        gate_jp = gates[:, 2 * LANES:3 * LANES] + gates[:, 3 * LANES:]
        beats = (((gate_jp > gate_j) | ((gate_jp == gate_j) & (trip_jp < trip_j)))
                 & (trip_jp < i) & trip_ok)
        return _dot(jnp.where(beats, 1.0, 0.0).astype(jnp.bfloat16), rank_sum)

    def score_stage(i, rank, out):
        keep = ((rank < MOBA_TOPK) & (bias_blk < i)) | (bias_blk == i) | ~bias_lane
        bias = jnp.where(keep, 0.0, MASK_NEG).astype(jnp.bfloat16)
        q_i = q_ref[i * MOBA_BLOCK:(i + 1) * MOBA_BLOCK, :]
        q_s = (q_i.astype(jnp.float32) * (HEAD_DIM ** -0.5 * LOG2_E)).astype(jnp.bfloat16)
        yield from _score_ops(jnp.where(own0, q_s, bias).T, ka_ref.at[0], i, causal, out[0])
        yield from _score_ops(jnp.where(own0, bias, q_s).T, ka_ref.at[1], i, causal, out[1])

    def value_stage(i, scores):
        out = []
        yield from _value_ops(scores[0], vt_ref.at[0], i, out)
        yield from _value_ops(scores[1], vt_ref.at[1], i, out)
        o_ref[i * MOBA_BLOCK:(i + 1) * MOBA_BLOCK, :] = (
            jnp.concatenate(out, axis=0).T.astype(o_ref.dtype))

    return _SlabStages(gate_stage, rank_stage, score_stage, value_stage)


def _moba(q, k, v):
    n_batch, seq, d_attn = q.shape
    assert seq % MOBA_BLOCK == 0 and d_attn % LANES == 0
    n_blocks = seq // MOBA_BLOCK
    width = MOBA_SLABS * LANES
    assert d_attn % width == 0
    spec = pl.BlockSpec((None, seq, width), lambda b, s: (b, 0, s))
    return pl.pallas_call(
        functools.partial(_moba_kernel, n_blocks=n_blocks),
        out_shape=jax.ShapeDtypeStruct((n_batch, seq, d_attn), jnp.bfloat16),
        grid=(n_batch, d_attn // width),
        in_specs=[spec, spec, spec],
        out_specs=spec,
        scratch_shapes=[
            pltpu.VMEM((MOBA_SLABS, LANES // HEAD_DIM, seq, LANES), jnp.bfloat16),
            pltpu.VMEM((MOBA_SLABS, LANES // HEAD_DIM, HEAD_DIM + BF16_SUBLANES, seq),
                       jnp.bfloat16)],
        compiler_params=pltpu.CompilerParams(
            dimension_semantics=("arbitrary", "arbitrary"), vmem_limit_bytes=VMEM_LIMIT_BYTES),
        name="moba",
    )(q, k, v)


def _out_ffn_kernel(x_ref, ys_ref, ya_ref, gattn_ref, wo_ref, gpost_ref, gpre_ref,
                    wg_ref, wu_ref, wd_ref, gffn_ref, o_ref, acc_ref):
    n_chunks = wg_ref.shape[1] // FF_CHUNK
    half = x_ref.shape[0] // 2
    d_ssm = ys_ref.shape[1]

    def mix_stage(r):
        rows = slice(r * half, (r + 1) * half)
        ya = _rms(ya_ref[rows, :].astype(jnp.float32), gattn_ref[...]).astype(jnp.bfloat16)
        mixed = _dot(ys_ref[rows, :], wo_ref[:d_ssm, :]) + _dot(ya, wo_ref[d_ssm:, :])
        x1 = x_ref[rows, :] + _rms(mixed, gpost_ref[...])
        return x1, _rms(x1, gpre_ref[...]).astype(jnp.bfloat16)

    def gate_up(h, c):
        lo, hi = c * FF_CHUNK, (c + 1) * FF_CHUNK
        return _dot(h, wg_ref[:, lo:hi]), _dot(h, wu_ref[:, lo:hi])

    def ffn_chunk(r, h, c, pending):
        gate, up = pending
        pending = gate_up(h, c + 1) if c + 1 < n_chunks else None
        f = (jax.nn.silu(gate) * up).astype(jnp.bfloat16)
        part = _dot(f, wd_ref[c * FF_CHUNK:(c + 1) * FF_CHUNK, :])
        if c == 0:
            acc_ref[r] = part
        else:
            acc_ref[r] += part
        return pending

    def out_stage(r, x1):
        o_ref[r * half:(r + 1) * half, :] = x1 + _rms(acc_ref[r], gffn_ref[...])

    x1_a, h_a = mix_stage(0)
    x1_b, h_b = mix_stage(1)
    pending = gate_up(h_a, 0)
    for c in range(n_chunks):
        pending = ffn_chunk(0, h_a, c, pending)
    pending = gate_up(h_b, 0)
    for c in range(n_chunks):
        pending = ffn_chunk(1, h_b, c, pending)
        if c == 0:
            out_stage(0, x1_a)
    out_stage(1, x1_b)


def _out_ffn(x2, ys2, ya2, g_attn, wo, g_post, g_pre, wg, wu, wd, g_ffn):
    n_rows, d_model = x2.shape
    d_half = ys2.shape[1]
    d_ff = wg.shape[1]
    assert n_rows % ROW_TILE == 0 and d_ff % FF_CHUNK == 0
    row = lambda i: (i, 0)
    resident = lambda a: pl.BlockSpec(a.shape, lambda i: (0,) * a.ndim, pipeline_mode=pl.Buffered(1))
    consts = (g_attn, wo, g_post, g_pre, wg, wu, wd, g_ffn)
    return pl.pallas_call(
        _out_ffn_kernel,
        out_shape=jax.ShapeDtypeStruct((n_rows, d_model), jnp.float32),
        grid=(n_rows // ROW_TILE,),
        in_specs=[
            pl.BlockSpec((ROW_TILE, d_model), row),
            pl.BlockSpec((ROW_TILE, d_half), row),
            pl.BlockSpec((ROW_TILE, d_half), row),
        ] + [resident(a) for a in consts],
        out_specs=pl.BlockSpec((ROW_TILE, d_model), row),
        scratch_shapes=[pltpu.VMEM((2, ROW_TILE // 2, d_model), jnp.float32)],
        compiler_params=pltpu.CompilerParams(
            dimension_semantics=("arbitrary",), vmem_limit_bytes=VMEM_LIMIT_BYTES),
        name="out_ffn",
    )(x2, ys2, ya2, *consts)


def _s5_params(a_re, a_im, log_dt, b_re, b_im, c_re, c_im, n_batch):
    f32 = jnp.float32
    a_re, a_im = a_re.astype(f32), a_im.astype(f32)
    b_re, b_im = b_re.astype(f32), b_im.astype(f32)
    n_groups = a_re.shape[0]
    n_clusters = n_groups // GROUPS_PER_CLUSTER
    dt = jnp.exp(log_dt.astype(f32))[:, None]
    mag = jnp.exp(a_re * dt)
    ang = a_im * dt
    lb_re = mag * jnp.cos(ang)
    lb_im = mag * jnp.sin(ang)
    nr = lb_re - 1.0
    den = a_re * a_re + a_im * a_im
    cr = (nr * a_re + lb_im * a_im) / den
    ci = (lb_im * a_re - nr * a_im) / den
    bb_re = cr[..., None] * b_re - ci[..., None] * b_im
    bb_im = cr[..., None] * b_im + ci[..., None] * b_re
    eye = jnp.eye(GROUPS_PER_CLUSTER, dtype=f32)

    def in_proj(bb):
        bb = bb.reshape(n_clusters, GROUPS_PER_CLUSTER, SSM_STATE, SSM_GROUP)
        m = jnp.einsum('cgph,gk->cghkp', bb, eye)
        return m.reshape(n_clusters, LANES, CLUSTER_STATE)

    def out_proj(cc):
        cc = cc.reshape(n_clusters, GROUPS_PER_CLUSTER, SSM_GROUP, SSM_STATE)
        m = jnp.einsum('cghp,gk->cgpkh', cc, eye)
        return m.reshape(n_clusters, CLUSTER_STATE, LANES)

    def rotate(m_re, m_im, l_re, l_im):
        return m_re * l_re - m_im * l_im, m_re * l_im + m_im * l_re

    def state_cols(m_re, m_im):
        return jnp.concatenate([in_proj(m_re), in_proj(m_im)], axis=2)

    def state_rows(m_re, m_im):
        return jnp.concatenate([out_proj(m_re), -out_proj(m_im)], axis=1)

    bl_re, bl_im = rotate(bb_re, bb_im, lb_re[..., None], lb_im[..., None])
    wv = jnp.concatenate([state_cols(bl_re, bl_im), state_cols(bb_re, bb_im)], axis=1)
    c_re, c_im = c_re.astype(f32), c_im.astype(f32)
    cl_re, cl_im = rotate(c_re, c_im, lb_re[:, None, :], lb_im[:, None, :])
    cz = jnp.concatenate([state_rows(c_re, c_im), state_rows(cl_re, cl_im)], axis=2)
    direct = (jnp.einsum('gop,gpi->gio', c_re, bb_re) - jnp.einsum('gop,gpi->gio', c_im, bb_im))
    direct = direct.reshape(n_clusters, GROUPS_PER_CLUSTER, SSM_GROUP, SSM_GROUP)
    dmat = jnp.einsum('cgio,gk->cgiko', direct, eye).reshape(n_clusters, LANES, LANES)
    l2_re, l2_im = rotate(lb_re, lb_im, lb_re, lb_im)
    lam2_re = jnp.broadcast_to(l2_re.reshape(1, -1), (n_batch, l2_re.size))
    lam2_im = jnp.broadcast_to(l2_im.reshape(1, -1), (n_batch, l2_im.size))
    bf16 = jnp.bfloat16
    return lam2_re, lam2_im, wv.astype(bf16), cz.astype(bf16), dmat.astype(bf16)


def _time_batch_permutation(n_batch):
    n = n_batch * PERM_T
    r = jnp.arange(n)
    parity, rest = r // (n // 2), r % (n // 2)
    src = (rest % n_batch) * PERM_T + 2 * (rest // n_batch) + parity
    return (src[:, None] == jnp.arange(n)[None, :]).astype(jnp.bfloat16)


def _layer(x, g_pre_mix, w_in, ssm_a_re, ssm_a_im, ssm_log_dt, ssm_b_re, ssm_b_im, ssm_c_re,
           ssm_c_im, ssm_d, w_glu, b_glu, g_ssm_out, g_attn_out, w_out, g_post_mix, g_pre_ffn,
           w_gate, w_up, w_down, g_post_ffn):
    n_batch, seq, d_model = x.shape
    d_ssm = w_glu.shape[0]
    bf16 = jnp.bfloat16
    row = lambda g: g.reshape(1, -1).astype(jnp.float32)
    lam_re, lam_im, wv, cz, dmat = _s5_params(ssm_a_re, ssm_a_im, ssm_log_dt, ssm_b_re, ssm_b_im,
                                              ssm_c_re, ssm_c_im, n_batch)
    perm = _time_batch_permutation(n_batch)
    q, k, v, y_ssm = _mixer_in(
        x, row(g_pre_mix), w_in, perm, perm.T, lam_re, lam_im, wv, cz, dmat, row(ssm_d), w_glu,
        row(b_glu), row(g_ssm_out))
    y_attn = _moba(q, k, v)
    n_rows = n_batch * seq
    out = _out_ffn(
        x.reshape(n_rows, d_model), y_ssm.reshape(n_rows, d_ssm), y_attn.reshape(n_rows, -1),
        row(g_attn_out), w_out.astype(bf16), row(g_post_mix), row(g_pre_ffn), w_gate.astype(bf16),
        w_up.astype(bf16), w_down.astype(bf16), row(g_post_ffn))
    return out.reshape(n_batch, seq, d_model)


def kernel(x, g_pre_mix, w_in, ssm_a_re, ssm_a_im, ssm_log_dt, ssm_b_re, ssm_b_im, ssm_c_re, ssm_c_im, ssm_d, w_glu, b_glu, g_ssm_out, g_attn_out, w_out, g_post_mix, g_pre_ffn, w_gate, w_up, w_down, g_post_ffn):
    for l in range(w_in.shape[0]):
        x = _layer(x, g_pre_mix[l], w_in[l], ssm_a_re[l], ssm_a_im[l], ssm_log_dt[l], ssm_b_re[l],
                   ssm_b_im[l], ssm_c_re[l], ssm_c_im[l], ssm_d[l], w_glu[l], b_glu[l],
                   g_ssm_out[l], g_attn_out[l], w_out[l], g_post_mix[l], g_pre_ffn[l], w_gate[l],
                   w_up[l], w_down[l], g_post_ffn[l])
    return x
```

```python
import collections
import functools
import math

import jax
import jax.numpy as jnp
from jax import lax
from jax.experimental import pallas as pl
from jax.experimental.pallas import tpu as pltpu

SSM_GROUP = 16
SSM_STATE = 64
HEAD_DIM = 64
MOBA_BLOCK = 256
MOBA_TOPK = 3
RMS_EPS = 1e-6

LANES = 128
BF16_SUBLANES = 16
VMEM_LIMIT_BYTES = 56 * 1024 * 1024

GROUPS_PER_CLUSTER = LANES // SSM_GROUP
CLUSTER_STATE = GROUPS_PER_CLUSTER * SSM_STATE
TIME_TILE = 64
PERM_T = BF16_SUBLANES
MOBA_SLABS = 2
ROW_TILE = 1024
FF_CHUNK = 256
MASK_NEG = -1e30
OVERFLOW_GUARD = 64.0
LOG2_E = math.log2(math.e)

_NT = (((1,), (1,)), ((), ()))
_SlabStages = collections.namedtuple('_SlabStages', 'gate_stage rank_stage score_stage value_stage')


def _rms(x, g):
    return x * lax.rsqrt(jnp.mean(x * x, axis=-1, keepdims=True) + RMS_EPS) * g


def _dot(a, b):
    return jnp.dot(a, b, preferred_element_type=jnp.float32)


def _dot_nt(a, b):
    return lax.dot_general(a, b, _NT, preferred_element_type=jnp.float32)


def _mixer_in_kernel(x_ref, gpre_ref, win_ref, perm_ref, permt_ref, lamre_ref, lamim_ref, wv_ref,
                     cz_ref, dmat_ref, dskip_ref, wglu_ref, bglu_ref, gssm_ref,
                     q_ref, k_ref, v_ref, yssm_ref,
                     state_ref, utm_ref, bu_ref, s_ref, y_ref, *, n_batch, n_clusters):
    lt = x_ref.shape[1]
    d_model = x_ref.shape[2]
    d_ssm = utm_ref.shape[2]
    rows = n_batch * lt
    half = rows // 2
    perm_rows = n_batch * PERM_T

    @pl.when(pl.program_id(0) == 0)
    def _():
        state_ref[...] = jnp.zeros_like(state_ref)

    x = x_ref[...].reshape(rows, d_model)
    h = _rms(x, gpre_ref[...]).astype(jnp.bfloat16)
    u = _dot(h, win_ref[:, 0:d_ssm]).astype(jnp.bfloat16)

    proj_cols = 2 * LANES
    projections = [(ref, col, tile) for ref, col in ((q_ref, 1), (k_ref, 2), (v_ref, 3))
                   for tile in range(d_ssm // proj_cols)]

    def project(count):
        for _ in range(min(count, len(projections))):
            ref, col, tile = projections.pop(0)
            c0 = col * d_ssm + tile * proj_cols
            z = _dot(h, win_ref[:, c0:c0 + proj_cols]).astype(jnp.bfloat16)
            ref[:, :, tile * proj_cols:(tile + 1) * proj_cols] = z.reshape(n_batch, lt, proj_cols)

    project(1)
    perm = perm_ref[...]
    pair_rows = perm_rows // 2
    for th in range(lt // PERM_T):
        chunk = jnp.concatenate(
            [u[b * lt + th * PERM_T: b * lt + (th + 1) * PERM_T] for b in range(n_batch)], axis=0)
        sorted_rows = _dot(perm, chunk).astype(jnp.bfloat16)
        for parity in range(2):
            utm_ref[parity, th * pair_rows:(th + 1) * pair_rows, :] = (
                sorted_rows[parity * pair_rows:(parity + 1) * pair_rows])
    for c in range(n_clusters):
        cols = slice(c * LANES, (c + 1) * LANES)
        bu_ref[c] = _dot(jnp.concatenate([utm_ref[0, :, cols], utm_ref[1, :, cols]], axis=1),
                         wv_ref[c])
        project(1)

    for c in range(n_clusters):
        lo, hi = c * CLUSTER_STATE, (c + 1) * CLUSTER_STATE
        cols = slice(c * LANES, (c + 1) * LANES)
        lam_re = lamre_ref[:, lo:hi]
        lam_im = lamim_ref[:, lo:hi]
        h_re = state_ref[0, :, lo:hi]
        h_im = state_ref[1, :, lo:hi]
        s_ref[c, 0:n_batch, 0:CLUSTER_STATE] = h_re.astype(jnp.bfloat16)
        s_ref[c, 0:n_batch, CLUSTER_STATE:2 * CLUSTER_STATE] = h_im.astype(jnp.bfloat16)
        for j in range(lt // 2):
            r0 = j * n_batch
            b_re = bu_ref[c, r0:r0 + n_batch, 0:CLUSTER_STATE]
            b_im = bu_ref[c, r0:r0 + n_batch, CLUSTER_STATE:2 * CLUSTER_STATE]
            n_re = lam_re * h_re - lam_im * h_im + b_re
            n_im = lam_re * h_im + lam_im * h_re + b_im
            h_re, h_im = n_re, n_im
            r1 = r0 + n_batch
            s_ref[c, r1:r1 + n_batch, 0:CLUSTER_STATE] = n_re.astype(jnp.bfloat16)
            s_ref[c, r1:r1 + n_batch, CLUSTER_STATE:2 * CLUSTER_STATE] = n_im.astype(jnp.bfloat16)
        state_ref[0, :, lo:hi] = h_re
        state_ref[1, :, lo:hi] = h_im
        z = _dot(s_ref[c], cz_ref[c])
        y_ref[0, :, cols] = z[:half, LANES:] + _dot(utm_ref[0, :, cols], dmat_ref[c])
        y_ref[1, :, cols] = z[n_batch:, :LANES]

    y = y_ref[...] + dskip_ref[...] * utm_ref[...].astype(jnp.float32)
    y = jax.nn.gelu(y.reshape(rows, d_ssm))
    glu = _dot(y.astype(jnp.bfloat16), wglu_ref[...])
    project(len(projections))
    y = y * jax.nn.sigmoid(glu + bglu_ref[...])
    yn = _rms(y, gssm_ref[...]).astype(jnp.bfloat16)
    perm_t = permt_ref[...]
    for th in range(lt // PERM_T):
        sorted_rows = jnp.concatenate(
            [yn[parity * half + th * pair_rows: parity * half + (th + 1) * pair_rows]
             for parity in range(2)], axis=0)
        back = _dot(perm_t, sorted_rows).astype(jnp.bfloat16)
        for b in range(n_batch):
            yssm_ref[b, th * PERM_T:(th + 1) * PERM_T, :] = back[b * PERM_T:(b + 1) * PERM_T]


def _mixer_in(x, g_pre, w_in, perm, perm_t, lam_re, lam_im, wv, cz, dmat, d_skip, w_glu, b_glu,
              g_ssm):
    n_batch, seq, d_model = x.shape
    d_ssm = w_glu.shape[0]
    n_clusters = d_ssm // LANES
    lt = TIME_TILE
    rows = n_batch * lt
    assert seq % lt == 0 and lt % PERM_T == 0 and n_batch % BF16_SUBLANES == 0
    resident = lambda a: pl.BlockSpec(a.shape, lambda i: (0,) * a.ndim, pipeline_mode=pl.Buffered(1))
    act_spec = pl.BlockSpec((n_batch, lt, d_ssm), lambda i: (0, i, 0))
    act_shape = jax.ShapeDtypeStruct((n_batch, seq, d_ssm), jnp.bfloat16)
    consts = (g_pre, w_in, perm, perm_t, lam_re, lam_im, wv, cz, dmat, d_skip, w_glu, b_glu, g_ssm)
    return pl.pallas_call(
        functools.partial(_mixer_in_kernel, n_batch=n_batch, n_clusters=n_clusters),
        out_shape=(act_shape,) * 4,
        grid=(seq // lt,),
        in_specs=[pl.BlockSpec((n_batch, lt, d_model), lambda i: (0, i, 0))]
        + [resident(a) for a in consts],
        out_specs=(act_spec,) * 4,
        scratch_shapes=[
            pltpu.VMEM((2, n_batch, n_clusters * CLUSTER_STATE), jnp.float32),
            pltpu.VMEM((2, rows // 2, d_ssm), jnp.bfloat16),
            pltpu.VMEM((n_clusters, rows // 2, 2 * CLUSTER_STATE), jnp.float32),
            pltpu.VMEM((n_clusters, rows // 2 + n_batch, 2 * CLUSTER_STATE), jnp.bfloat16),
            pltpu.VMEM((2, rows // 2, d_ssm), jnp.float32),
        ],
        compiler_params=pltpu.CompilerParams(
            dimension_semantics=("arbitrary",), vmem_limit_bytes=VMEM_LIMIT_BYTES),
        name="mixer_in",
    )(x, *consts)


def _block_order(i):
    return [i] + list(range(i))


def _score_ops(q_aug_t, ka_ref, i, causal, out):
    for j in _block_order(i):
        s = _dot(ka_ref[j * MOBA_BLOCK:(j + 1) * MOBA_BLOCK, :], q_aug_t)
        out.append(jnp.where(causal, s, MASK_NEG) if j == i else s)
        yield


def _prob_ops(q_aug_t, ka_ref, i, causal, out, excess):
    m_own = m_past = None
    for j in _block_order(i):
        s = _dot(ka_ref[j * MOBA_BLOCK:(j + 1) * MOBA_BLOCK, :], q_aug_t)
        if j == i:
            s = jnp.where(causal, s, MASK_NEG)
            m_own = jnp.max(s, axis=0, keepdims=True)
        else:
            m_blk = jnp.max(s, axis=0, keepdims=True)
            m_past = m_blk if m_past is None else jnp.maximum(m_past, m_blk)
        out.append(jnp.exp2(s - m_own).astype(jnp.bfloat16))
        yield
    if m_past is not None:
        excess.append(m_past - m_own)


def _value_ops(scores, vt_ref, i, out, normalised):
    if not normalised:
        m = functools.reduce(jnp.maximum, [jnp.max(s, axis=0, keepdims=True) for s in scores])
    acc = None
    for j, s in zip(_block_order(i), scores):
        p = s if normalised else jnp.exp2(s - m).astype(jnp.bfloat16)
        part = _dot(vt_ref[:, j * MOBA_BLOCK:(j + 1) * MOBA_BLOCK], p)
        acc = part if acc is None else acc + part
        yield
    out.append(acc[:HEAD_DIM] / acc[HEAD_DIM:HEAD_DIM + 1])


def _interleave(*generators):
    live = list(generators)
    while live:
        for g in list(live):
            try:
                next(g)
            except StopIteration:
                live.remove(g)


def _moba_kernel(q_ref, k_ref, v_ref, o_ref, ka_ref, vt_ref, *, n_blocks):
    slabs = []
    for s in range(q_ref.shape[1] // LANES):
        cols = slice(s * LANES, (s + 1) * LANES)
        slabs.append(_moba_slab(q_ref.at[:, cols], k_ref.at[:, cols], v_ref.at[:, cols],
                                o_ref.at[:, cols], ka_ref.at[s], vt_ref.at[s], n_blocks))

    excess = []
    _run_stages([slab(excess) for slab in slabs], n_blocks)
    if excess:
        worst = jnp.max(functools.reduce(jnp.maximum, excess))

        @pl.when(worst > OVERFLOW_GUARD)
        def _():
            _run_stages([slab(None) for slab in slabs], n_blocks)


def _run_stages(stages, n_blocks):
    items = [(s, i) for s in range(len(stages)) for i in range(n_blocks - 1, -1, -1)]
    gates, ranks, scores = {}, {}, {}
    for step in range(len(items) + 3):
        if step < len(items):
            s, i = items[step]
            gates[s, i] = stages[s].gate_stage(i)
        if 0 <= step - 1 < len(items):
            s, i = items[step - 1]
            ranks[s, i] = stages[s].rank_stage(i, gates.pop((s, i)))
        pending = []
        if 0 <= step - 2 < len(items):
            s, i = items[step - 2]
            scores[s, i] = ([], [])
            pending.append(stages[s].score_stage(i, ranks.pop((s, i)), scores[s, i]))
        if 0 <= step - 3 < len(items):
            s, i = items[step - 3]
            pending.append(stages[s].value_stage(i, scores.pop((s, i))))
        _interleave(*pending)


def _moba_slab(q_ref, k_ref, v_ref, o_ref, ka_ref, vt_ref, n_blocks):
    seq = q_ref.shape[0]
    n_pairs = n_blocks * n_blocks
    assert LANES == 2 * HEAD_DIM and n_pairs <= HEAD_DIM
    lane = lax.broadcasted_iota(jnp.int32, (1, LANES), 1)
    own0 = lane < HEAD_DIM
    causal = (lax.broadcasted_iota(jnp.int32, (MOBA_BLOCK, MOBA_BLOCK), 0)
              <= lax.broadcasted_iota(jnp.int32, (MOBA_BLOCK, MOBA_BLOCK), 1))

    key_blk = lax.broadcasted_iota(jnp.int32, (seq, 1), 0) // MOBA_BLOCK
    lane_full = lax.broadcasted_iota(jnp.int32, (seq, LANES), 1)
    k_all = k_ref[...]
    ka_ref[0] = jnp.where(own0, k_all,
                          jnp.where(lane_full == key_blk + HEAD_DIM, 1.0, 0.0).astype(jnp.bfloat16))
    ka_ref[1] = jnp.where(own0, jnp.where(lane_full == key_blk, 1.0, 0.0).astype(jnp.bfloat16), k_all)
    v_t = v_ref[...].T
    ones_rows = jnp.where(lax.broadcasted_iota(jnp.int32, (BF16_SUBLANES, seq), 0) == 0, 1.0, 0.0)
    ones_rows = ones_rows.astype(jnp.bfloat16)
    vt_ref[0] = jnp.concatenate([v_t[:HEAD_DIM], ones_rows], axis=0)
    vt_ref[1] = jnp.concatenate([v_t[HEAD_DIM:], ones_rows], axis=0)

    k_mean = jnp.sum(k_all.astype(jnp.float32).reshape(n_blocks, MOBA_BLOCK, LANES), axis=1)
    k_mean = k_mean * (1.0 / MOBA_BLOCK)
    km_a = jnp.broadcast_to(k_mean[:, None, :], (n_blocks, n_blocks, LANES)).reshape(n_pairs, LANES)
    km_b = jnp.broadcast_to(k_mean[None, :, :], (n_blocks, n_blocks, LANES)).reshape(n_pairs, LANES)
    pad = [jnp.zeros((HEAD_DIM - n_pairs, LANES), jnp.float32)] if n_pairs < HEAD_DIM else []

    def both_heads(km):
        return jnp.concatenate(
            [jnp.where(own0, km, 0.0)] + pad + [jnp.where(own0, 0.0, km)] + pad, axis=0)

    def split(m):
        hi = m.astype(jnp.bfloat16)
        return [hi, (m - hi.astype(jnp.float32)).astype(jnp.bfloat16)]
    gate_rhs = jnp.concatenate(split(both_heads(km_a)) + split(both_heads(km_b)), axis=0)

    trip = lane % HEAD_DIM
    trip_j, trip_jp = trip // n_blocks, trip % n_blocks
    trip_ok = trip < n_pairs
    r_idx = lax.broadcasted_iota(jnp.int32, (LANES, LANES), 0)
    c_idx = lax.broadcasted_iota(jnp.int32, (LANES, LANES), 1)
    r_bias_lane = jnp.where(r_idx < HEAD_DIM, HEAD_DIM, 0) + (r_idx % HEAD_DIM) // n_blocks
    rank_sum = jnp.where((c_idx == r_bias_lane) & (r_idx % HEAD_DIM < n_pairs), 1.0, 0.0)
    rank_sum = rank_sum.astype(jnp.bfloat16)
    bias_blk = jnp.where(own0, lane, lane - HEAD_DIM)
    bias_lane = bias_blk < n_blocks

    def gate_stage(i):
        q_i = q_ref[i * MOBA_BLOCK:(i + 1) * MOBA_BLOCK, :]
        return _dot_nt(q_i * jnp.bfloat16(HEAD_DIM ** -0.5), gate_rhs)

    def rank_stage(i, gates):
        gate_j = gates[:, 0:LANES] + gates[:, LANES:2 * LANES]
        gate_jp = gates[:, 2 * LANES:3 * LANES] + gates[:, 3 * LANES:]
        beats = (((gate_jp > gate_j) | ((gate_jp == gate_j) & (trip_jp < trip_j)))
                 & (trip_jp < i) & trip_ok)
        return _dot(jnp.where(beats, 1.0, 0.0).astype(jnp.bfloat16), rank_sum)

    def make_stages(excess):
        single_pass = excess is not None

        def score_stage(i, rank, out):
            keep = ((rank < MOBA_TOPK) & (bias_blk < i)) | (bias_blk == i) | ~bias_lane
            bias = jnp.where(keep, 0.0, MASK_NEG).astype(jnp.bfloat16)
            q_i = q_ref[i * MOBA_BLOCK:(i + 1) * MOBA_BLOCK, :]
            q_s = (q_i.astype(jnp.float32) * (HEAD_DIM ** -0.5 * LOG2_E)).astype(jnp.bfloat16)
            for hd, q_aug in enumerate((jnp.where(own0, q_s, bias), jnp.where(own0, bias, q_s))):
                if single_pass:
                    yield from _prob_ops(q_aug.T, ka_ref.at[hd], i, causal, out[hd], excess)
                else:
                    yield from _score_ops(q_aug.T, ka_ref.at[hd], i, causal, out[hd])

        def value_stage(i, scores):
            out = []
            yield from _value_ops(scores[0], vt_ref.at[0], i, out, single_pass)
            yield from _value_ops(scores[1], vt_ref.at[1], i, out, single_pass)
            o_ref[i * MOBA_BLOCK:(i + 1) * MOBA_BLOCK, :] = (
                jnp.concatenate(out, axis=0).T.astype(o_ref.dtype))

        return _SlabStages(gate_stage, rank_stage, score_stage, value_stage)

    return make_stages


def _moba(q, k, v):
    n_batch, seq, d_attn = q.shape
    assert seq % MOBA_BLOCK == 0 and d_attn % LANES == 0
    n_blocks = seq // MOBA_BLOCK
    width = MOBA_SLABS * LANES
    assert d_attn % width == 0
    spec = pl.BlockSpec((None, seq, width), lambda b, s: (b, 0, s))
    return pl.pallas_call(
        functools.partial(_moba_kernel, n_blocks=n_blocks),
        out_shape=jax.ShapeDtypeStruct((n_batch, seq, d_attn), jnp.bfloat16),
        grid=(n_batch, d_attn // width),
        in_specs=[spec, spec, spec],
        out_specs=spec,
        scratch_shapes=[
            pltpu.VMEM((MOBA_SLABS, LANES // HEAD_DIM, seq, LANES), jnp.bfloat16),
            pltpu.VMEM((MOBA_SLABS, LANES // HEAD_DIM, HEAD_DIM + BF16_SUBLANES, seq),
                       jnp.bfloat16)],
        compiler_params=pltpu.CompilerParams(
            dimension_semantics=("arbitrary", "arbitrary"), vmem_limit_bytes=VMEM_LIMIT_BYTES),
        name="moba",
    )(q, k, v)


def _out_ffn_kernel(x_ref, ys_ref, ya_ref, gattn_ref, wo_ref, gpost_ref, gpre_ref,
                    wg_ref, wu_ref, wd_ref, gffn_ref, o_ref, acc_ref):
    n_chunks = wg_ref.shape[1] // FF_CHUNK
    half = x_ref.shape[0] // 2
    d_ssm = ys_ref.shape[1]

    def mix_stage(r):
        rows = slice(r * half, (r + 1) * half)
        ya = _rms(ya_ref[rows, :].astype(jnp.float32), gattn_ref[...]).astype(jnp.bfloat16)
        mixed = _dot(ys_ref[rows, :], wo_ref[:d_ssm, :]) + _dot(ya, wo_ref[d_ssm:, :])
        x1 = x_ref[rows, :] + _rms(mixed, gpost_ref[...])
        return x1, _rms(x1, gpre_ref[...]).astype(jnp.bfloat16)

    def gate_up(h, c):
        lo, hi = c * FF_CHUNK, (c + 1) * FF_CHUNK
        return _dot(h, wg_ref[:, lo:hi]), _dot(h, wu_ref[:, lo:hi])

    def ffn_chunk(r, h, c, pending):
        gate, up = pending
        pending = gate_up(h, c + 1) if c + 1 < n_chunks else None
        f = (jax.nn.silu(gate) * up).astype(jnp.bfloat16)
        part = _dot(f, wd_ref[c * FF_CHUNK:(c + 1) * FF_CHUNK, :])
        if c == 0:
            acc_ref[r] = part
        else:
            acc_ref[r] += part
        return pending

    def out_stage(r, x1):
        o_ref[r * half:(r + 1) * half, :] = x1 + _rms(acc_ref[r], gffn_ref[...])

    x1_a, h_a = mix_stage(0)
    x1_b, h_b = mix_stage(1)
    pending = gate_up(h_a, 0)
    for c in range(n_chunks):
        pending = ffn_chunk(0, h_a, c, pending)
    pending = gate_up(h_b, 0)
    for c in range(n_chunks):
        pending = ffn_chunk(1, h_b, c, pending)
        if c == 0:
            out_stage(0, x1_a)
    out_stage(1, x1_b)


def _out_ffn(x2, ys2, ya2, g_attn, wo, g_post, g_pre, wg, wu, wd, g_ffn):
    n_rows, d_model = x2.shape
    d_half = ys2.shape[1]
    d_ff = wg.shape[1]
    assert n_rows % ROW_TILE == 0 and d_ff % FF_CHUNK == 0
    row = lambda i: (i, 0)
    resident = lambda a: pl.BlockSpec(a.shape, lambda i: (0,) * a.ndim, pipeline_mode=pl.Buffered(1))
    consts = (g_attn, wo, g_post, g_pre, wg, wu, wd, g_ffn)
    return pl.pallas_call(
        _out_ffn_kernel,
        out_shape=jax.ShapeDtypeStruct((n_rows, d_model), jnp.float32),
        grid=(n_rows // ROW_TILE,),
        in_specs=[
            pl.BlockSpec((ROW_TILE, d_model), row),
            pl.BlockSpec((ROW_TILE, d_half), row),
            pl.BlockSpec((ROW_TILE, d_half), row),
        ] + [resident(a) for a in consts],
        out_specs=pl.BlockSpec((ROW_TILE, d_model), row),
        scratch_shapes=[pltpu.VMEM((2, ROW_TILE // 2, d_model), jnp.float32)],
        compiler_params=pltpu.CompilerParams(
            dimension_semantics=("arbitrary",), vmem_limit_bytes=VMEM_LIMIT_BYTES),
        name="out_ffn",
    )(x2, ys2, ya2, *consts)


def _s5_params(a_re, a_im, log_dt, b_re, b_im, c_re, c_im, n_batch):
    f32 = jnp.float32
    a_re, a_im = a_re.astype(f32), a_im.astype(f32)
    b_re, b_im = b_re.astype(f32), b_im.astype(f32)
    n_groups = a_re.shape[0]
    n_clusters = n_groups // GROUPS_PER_CLUSTER
    dt = jnp.exp(log_dt.astype(f32))[:, None]
    mag = jnp.exp(a_re * dt)
    ang = a_im * dt
    lb_re = mag * jnp.cos(ang)
    lb_im = mag * jnp.sin(ang)
    nr = lb_re - 1.0
    den = a_re * a_re + a_im * a_im
    cr = (nr * a_re + lb_im * a_im) / den
    ci = (lb_im * a_re - nr * a_im) / den
    bb_re = cr[..., None] * b_re - ci[..., None] * b_im
    bb_im = cr[..., None] * b_im + ci[..., None] * b_re
    eye = jnp.eye(GROUPS_PER_CLUSTER, dtype=f32)

    def in_proj(bb):
        bb = bb.reshape(n_clusters, GROUPS_PER_CLUSTER, SSM_STATE, SSM_GROUP)
        m = jnp.einsum('cgph,gk->cghkp', bb, eye)
        return m.reshape(n_clusters, LANES, CLUSTER_STATE)

    def out_proj(cc):
        cc = cc.reshape(n_clusters, GROUPS_PER_CLUSTER, SSM_GROUP, SSM_STATE)
        m = jnp.einsum('cghp,gk->cgpkh', cc, eye)
        return m.reshape(n_clusters, CLUSTER_STATE, LANES)

    def rotate(m_re, m_im, l_re, l_im):
        return m_re * l_re - m_im * l_im, m_re * l_im + m_im * l_re

    def state_cols(m_re, m_im):
        return jnp.concatenate([in_proj(m_re), in_proj(m_im)], axis=2)

    def state_rows(m_re, m_im):
        return jnp.concatenate([out_proj(m_re), -out_proj(m_im)], axis=1)

    bl_re, bl_im = rotate(bb_re, bb_im, lb_re[..., None], lb_im[..., None])
    wv = jnp.concatenate([state_cols(bl_re, bl_im), state_cols(bb_re, bb_im)], axis=1)
    c_re, c_im = c_re.astype(f32), c_im.astype(f32)
    cl_re, cl_im = rotate(c_re, c_im, lb_re[:, None, :], lb_im[:, None, :])
    cz = jnp.concatenate([state_rows(c_re, c_im), state_rows(cl_re, cl_im)], axis=2)
    direct = (jnp.einsum('gop,gpi->gio', c_re, bb_re) - jnp.einsum('gop,gpi->gio', c_im, bb_im))
    direct = direct.reshape(n_clusters, GROUPS_PER_CLUSTER, SSM_GROUP, SSM_GROUP)
    dmat = jnp.einsum('cgio,gk->cgiko', direct, eye).reshape(n_clusters, LANES, LANES)
    l2_re, l2_im = rotate(lb_re, lb_im, lb_re, lb_im)
    lam2_re = jnp.broadcast_to(l2_re.reshape(1, -1), (n_batch, l2_re.size))
    lam2_im = jnp.broadcast_to(l2_im.reshape(1, -1), (n_batch, l2_im.size))
    bf16 = jnp.bfloat16
    return lam2_re, lam2_im, wv.astype(bf16), cz.astype(bf16), dmat.astype(bf16)


def _time_batch_permutation(n_batch):
    n = n_batch * PERM_T
    r = jnp.arange(n)
    parity, rest = r // (n // 2), r % (n // 2)
    src = (rest % n_batch) * PERM_T + 2 * (rest // n_batch) + parity
    return (src[:, None] == jnp.arange(n)[None, :]).astype(jnp.bfloat16)


def _layer(x, g_pre_mix, w_in, ssm_a_re, ssm_a_im, ssm_log_dt, ssm_b_re, ssm_b_im, ssm_c_re,
           ssm_c_im, ssm_d, w_glu, b_glu, g_ssm_out, g_attn_out, w_out, g_post_mix, g_pre_ffn,
           w_gate, w_up, w_down, g_post_ffn):
    n_batch, seq, d_model = x.shape
    d_ssm = w_glu.shape[0]
    bf16 = jnp.bfloat16
    row = lambda g: g.reshape(1, -1).astype(jnp.float32)
    lam_re, lam_im, wv, cz, dmat = _s5_params(ssm_a_re, ssm_a_im, ssm_log_dt, ssm_b_re, ssm_b_im,
                                              ssm_c_re, ssm_c_im, n_batch)
    perm = _time_batch_permutation(n_batch)
    q, k, v, y_ssm = _mixer_in(
        x, row(g_pre_mix), w_in, perm, perm.T, lam_re, lam_im, wv, cz, dmat, row(ssm_d), w_glu,
        row(b_glu), row(g_ssm_out))
    y_attn = _moba(q, k, v)
    n_rows = n_batch * seq
    out = _out_ffn(
        x.reshape(n_rows, d_model), y_ssm.reshape(n_rows, d_ssm), y_attn.reshape(n_rows, -1),
        row(g_attn_out), w_out.astype(bf16), row(g_post_mix), row(g_pre_ffn), w_gate.astype(bf16),
        w_up.astype(bf16), w_down.astype(bf16), row(g_post_ffn))
    return out.reshape(n_batch, seq, d_model)


def kernel(x, g_pre_mix, w_in, ssm_a_re, ssm_a_im, ssm_log_dt, ssm_b_re, ssm_b_im, ssm_c_re, ssm_c_im, ssm_d, w_glu, b_glu, g_ssm_out, g_attn_out, w_out, g_post_mix, g_pre_ffn, w_gate, w_up, w_down, g_post_ffn):
    for l in range(w_in.shape[0]):
        x = _layer(x, g_pre_mix[l], w_in[l], ssm_a_re[l], ssm_a_im[l], ssm_log_dt[l], ssm_b_re[l],
                   ssm_b_im[l], ssm_c_re[l], ssm_c_im[l], ssm_d[l], w_glu[l], b_glu[l],
                   g_ssm_out[l], g_attn_out[l], w_out[l], g_post_mix[l], g_pre_ffn[l], w_gate[l],
                   w_up[l], w_down[l], g_post_ffn[l])
    return x
```

```python
import collections
import functools
import math

import jax
import jax.numpy as jnp
from jax import lax
from jax.experimental import pallas as pl
from jax.experimental.pallas import tpu as pltpu

SSM_GROUP = 16
SSM_STATE = 64
HEAD_DIM = 64
MOBA_BLOCK = 256
MOBA_TOPK = 3
RMS_EPS = 1e-6

LANES = 128
BF16_SUBLANES = 16
VMEM_LIMIT_BYTES = 56 * 1024 * 1024

GROUPS_PER_CLUSTER = LANES // SSM_GROUP
CLUSTER_STATE = GROUPS_PER_CLUSTER * SSM_STATE
TIME_TILE = 64
PERM_T = BF16_SUBLANES
MOBA_SLABS = 2
ROW_TILE = 1024
FF_CHUNK = 256
MASK_NEG = -1e30
LOG2_E = math.log2(math.e)

_NT = (((1,), (1,)), ((), ()))
_SlabStages = collections.namedtuple('_SlabStages', 'gate_stage rank_stage score_stage value_stage')


def _rms(x, g):
    return x * lax.rsqrt(jnp.mean(x * x, axis=-1, keepdims=True) + RMS_EPS) * g


def _dot(a, b):
    return jnp.dot(a, b, preferred_element_type=jnp.float32)


def _dot_nt(a, b):
    return lax.dot_general(a, b, _NT, preferred_element_type=jnp.float32)


def _mixer_in_kernel(x_ref, gpre_ref, win_ref, perm_ref, permt_ref, lamre_ref, lamim_ref, wv_ref,
                     cz_ref, dmat_ref, dskip_ref, wglu_ref, bglu_ref, gssm_ref,
                     q_ref, k_ref, v_ref, yssm_ref,
                     state_ref, utm_ref, bu_ref, s_ref, y_ref, *, n_batch, n_clusters):
    lt = x_ref.shape[1]
    d_model = x_ref.shape[2]
    d_ssm = utm_ref.shape[2]
    rows = n_batch * lt
    half = rows // 2
    perm_rows = n_batch * PERM_T

    @pl.when(pl.program_id(0) == 0)
    def _():
        state_ref[...] = jnp.zeros_like(state_ref)

    x = x_ref[...].reshape(rows, d_model)
    h = _rms(x, gpre_ref[...]).astype(jnp.bfloat16)
    u = _dot(h, win_ref[:, 0:d_ssm]).astype(jnp.bfloat16)

    proj_cols = 2 * LANES
    projections = [(ref, col, tile) for ref, col in ((q_ref, 1), (k_ref, 2), (v_ref, 3))
                   for tile in range(d_ssm // proj_cols)]

    def project(count):
        for _ in range(min(count, len(projections))):
            ref, col, tile = projections.pop(0)
            c0 = col * d_ssm + tile * proj_cols
            z = _dot(h, win_ref[:, c0:c0 + proj_cols]).astype(jnp.bfloat16)
            ref[:, :, tile * proj_cols:(tile + 1) * proj_cols] = z.reshape(n_batch, lt, proj_cols)

    project(1)
    perm = perm_ref[...]
    pair_rows = perm_rows // 2
    for th in range(lt // PERM_T):
        chunk = jnp.concatenate(
            [u[b * lt + th * PERM_T: b * lt + (th + 1) * PERM_T] for b in range(n_batch)], axis=0)
        sorted_rows = _dot(perm, chunk).astype(jnp.bfloat16)
        for parity in range(2):
            utm_ref[parity, th * pair_rows:(th + 1) * pair_rows, :] = (
                sorted_rows[parity * pair_rows:(parity + 1) * pair_rows])
    for c in range(n_clusters):
        cols = slice(c * LANES, (c + 1) * LANES)
        bu_ref[c] = _dot(jnp.concatenate([utm_ref[0, :, cols], utm_ref[1, :, cols]], axis=1),
                         wv_ref[c])
        project(1)

    for c in range(n_clusters):
        lo, hi = c * CLUSTER_STATE, (c + 1) * CLUSTER_STATE
        cols = slice(c * LANES, (c + 1) * LANES)
        lam_re = lamre_ref[:, lo:hi]
        lam_im = lamim_ref[:, lo:hi]
        h_re = state_ref[0, :, lo:hi]
        h_im = state_ref[1, :, lo:hi]
        s_ref[c, 0:n_batch, 0:CLUSTER_STATE] = h_re.astype(jnp.bfloat16)
        s_ref[c, 0:n_batch, CLUSTER_STATE:2 * CLUSTER_STATE] = h_im.astype(jnp.bfloat16)
        for j in range(lt // 2):
            r0 = j * n_batch
            b_re = bu_ref[c, r0:r0 + n_batch, 0:CLUSTER_STATE]
            b_im = bu_ref[c, r0:r0 + n_batch, CLUSTER_STATE:2 * CLUSTER_STATE]
            n_re = lam_re * h_re - lam_im * h_im + b_re
            n_im = lam_re * h_im + lam_im * h_re + b_im
            h_re, h_im = n_re, n_im
            r1 = r0 + n_batch
            s_ref[c, r1:r1 + n_batch, 0:CLUSTER_STATE] = n_re.astype(jnp.bfloat16)
            s_ref[c, r1:r1 + n_batch, CLUSTER_STATE:2 * CLUSTER_STATE] = n_im.astype(jnp.bfloat16)
        state_ref[0, :, lo:hi] = h_re
        state_ref[1, :, lo:hi] = h_im
        z = _dot(s_ref[c], cz_ref[c])
        y_ref[0, :, cols] = z[:half, LANES:] + _dot(utm_ref[0, :, cols], dmat_ref[c])
        y_ref[1, :, cols] = z[n_batch:, :LANES]

    perm_t = permt_ref[...]
    n_groups = lt // PERM_T
    half_groups = n_groups // 2

    def glu_stage(g0):
        span = slice(g0 * pair_rows, (g0 + half_groups) * pair_rows)
        y = jnp.concatenate(
            [y_ref[parity, span, :] + dskip_ref[...] * utm_ref[parity, span, :].astype(jnp.float32)
             for parity in range(2)], axis=0)
        y = jax.nn.gelu(y)
        return y, _dot(y.astype(jnp.bfloat16), wglu_ref[...])

    def store_stage(g0, y, glu):
        y = y * jax.nn.sigmoid(glu + bglu_ref[...])
        yn = _rms(y, gssm_ref[...]).astype(jnp.bfloat16)
        quarter = half_groups * pair_rows
        for g in range(half_groups):
            sorted_rows = jnp.concatenate(
                [yn[parity * quarter + g * pair_rows: parity * quarter + (g + 1) * pair_rows]
                 for parity in range(2)], axis=0)
            back = _dot(perm_t, sorted_rows).astype(jnp.bfloat16)
            th = g0 + g
            for b in range(n_batch):
                yssm_ref[b, th * PERM_T:(th + 1) * PERM_T, :] = back[b * PERM_T:(b + 1) * PERM_T]

    first = glu_stage(0)
    project(len(projections))
    second = glu_stage(half_groups)
    store_stage(0, *first)
    store_stage(half_groups, *second)


def _mixer_in(x, g_pre, w_in, perm, perm_t, lam_re, lam_im, wv, cz, dmat, d_skip, w_glu, b_glu,
              g_ssm):
    n_batch, seq, d_model = x.shape
    d_ssm = w_glu.shape[0]
    n_clusters = d_ssm // LANES
    lt = TIME_TILE
    rows = n_batch * lt
    assert seq % lt == 0 and lt % PERM_T == 0 and n_batch % BF16_SUBLANES == 0
    resident = lambda a: pl.BlockSpec(a.shape, lambda i: (0,) * a.ndim, pipeline_mode=pl.Buffered(1))
    act_spec = pl.BlockSpec((n_batch, lt, d_ssm), lambda i: (0, i, 0))
    act_shape = jax.ShapeDtypeStruct((n_batch, seq, d_ssm), jnp.bfloat16)
    consts = (g_pre, w_in, perm, perm_t, lam_re, lam_im, wv, cz, dmat, d_skip, w_glu, b_glu, g_ssm)
    return pl.pallas_call(
        functools.partial(_mixer_in_kernel, n_batch=n_batch, n_clusters=n_clusters),
        out_shape=(act_shape,) * 4,
        grid=(seq // lt,),
        in_specs=[pl.BlockSpec((n_batch, lt, d_model), lambda i: (0, i, 0))]
        + [resident(a) for a in consts],
        out_specs=(act_spec,) * 4,
        scratch_shapes=[
            pltpu.VMEM((2, n_batch, n_clusters * CLUSTER_STATE), jnp.float32),
            pltpu.VMEM((2, rows // 2, d_ssm), jnp.bfloat16),
            pltpu.VMEM((n_clusters, rows // 2, 2 * CLUSTER_STATE), jnp.float32),
            pltpu.VMEM((n_clusters, rows // 2 + n_batch, 2 * CLUSTER_STATE), jnp.bfloat16),
            pltpu.VMEM((2, rows // 2, d_ssm), jnp.float32),
        ],
        compiler_params=pltpu.CompilerParams(
            dimension_semantics=("arbitrary",), vmem_limit_bytes=VMEM_LIMIT_BYTES),
        name="mixer_in",
    )(x, *consts)


def _block_order(i):
    return [i] + list(range(i))


def _score_ops(q_aug_t, ka_ref, i, causal, out):
    for j in _block_order(i):
        s = _dot(ka_ref[j * MOBA_BLOCK:(j + 1) * MOBA_BLOCK, :], q_aug_t)
        out.append(jnp.where(causal, s, MASK_NEG) if j == i else s)
        yield


def _prob_ops(q_aug_t, ka_ref, i, causal, out):
    m_own = None
    for j in _block_order(i):
        s = _dot(ka_ref[j * MOBA_BLOCK:(j + 1) * MOBA_BLOCK, :], q_aug_t)
        if j == i:
            s = jnp.where(causal, s, MASK_NEG)
            m_own = jnp.max(s, axis=0, keepdims=True)
        out.append(jnp.exp2(s - m_own).astype(jnp.bfloat16))
        yield


def _value_ops(scores, vt_ref, i, out, overflow):
    normalised = overflow is not None
    if not normalised:
        m = functools.reduce(jnp.maximum, [jnp.max(s, axis=0, keepdims=True) for s in scores])
    acc = None
    for j, s in zip(_block_order(i), scores):
        p = s if normalised else jnp.exp2(s - m).astype(jnp.bfloat16)
        part = _dot(vt_ref[:, j * MOBA_BLOCK:(j + 1) * MOBA_BLOCK], p)
        acc = part if acc is None else acc + part
        yield
    if normalised:
        overflow.append(jnp.where(jnp.isfinite(acc), 0.0, 1.0))
    out.append(acc[:HEAD_DIM] / acc[HEAD_DIM:HEAD_DIM + 1])


def _interleave(*generators):
    live = list(generators)
    while live:
        for g in list(live):
            try:
                next(g)
            except StopIteration:
                live.remove(g)


def _moba_kernel(q_ref, k_ref, v_ref, o_ref, ka_ref, vt_ref, *, n_blocks):
    slabs = []
    for s in range(q_ref.shape[1] // LANES):
        cols = slice(s * LANES, (s + 1) * LANES)
        slabs.append(_moba_slab(q_ref.at[:, cols], k_ref.at[:, cols], v_ref.at[:, cols],
                                o_ref.at[:, cols], ka_ref.at[s], vt_ref.at[s], n_blocks))

    overflow = []
    _run_stages([slab(overflow) for slab in slabs], n_blocks)
    overflowed = jnp.max(functools.reduce(jnp.maximum, overflow)) > 0.0

    @pl.when(overflowed)
    def _():
        _run_stages([slab(None) for slab in slabs], n_blocks)


def _run_stages(stages, n_blocks):
    items = [(s, i) for s in range(len(stages)) for i in range(n_blocks - 1, -1, -1)]
    gates, ranks, scores = {}, {}, {}
    for step in range(len(items) + 3):
        if step < len(items):
            s, i = items[step]
            gates[s, i] = stages[s].gate_stage(i)
        if 0 <= step - 1 < len(items):
            s, i = items[step - 1]
            ranks[s, i] = stages[s].rank_stage(i, gates.pop((s, i)))
        pending = []
        if 0 <= step - 2 < len(items):
            s, i = items[step - 2]
            scores[s, i] = ([], [])
            pending.append(stages[s].score_stage(i, ranks.pop((s, i)), scores[s, i]))
        if 0 <= step - 3 < len(items):
            s, i = items[step - 3]
            pending.append(stages[s].value_stage(i, scores.pop((s, i))))
        _interleave(*pending)


def _moba_slab(q_ref, k_ref, v_ref, o_ref, ka_ref, vt_ref, n_blocks):
    seq = q_ref.shape[0]
    n_pairs = n_blocks * n_blocks
    assert LANES == 2 * HEAD_DIM and n_pairs <= HEAD_DIM
    lane = lax.broadcasted_iota(jnp.int32, (1, LANES), 1)
    own0 = lane < HEAD_DIM
    causal = (lax.broadcasted_iota(jnp.int32, (MOBA_BLOCK, MOBA_BLOCK), 0)
              <= lax.broadcasted_iota(jnp.int32, (MOBA_BLOCK, MOBA_BLOCK), 1))

    key_blk = lax.broadcasted_iota(jnp.int32, (seq, 1), 0) // MOBA_BLOCK
    lane_full = lax.broadcasted_iota(jnp.int32, (seq, LANES), 1)
    k_all = k_ref[...]
    ka_ref[0] = jnp.where(own0, k_all,
                          jnp.where(lane_full == key_blk + HEAD_DIM, 1.0, 0.0).astype(jnp.bfloat16))
    ka_ref[1] = jnp.where(own0, jnp.where(lane_full == key_blk, 1.0, 0.0).astype(jnp.bfloat16), k_all)
    v_t = v_ref[...].T
    ones_rows = jnp.where(lax.broadcasted_iota(jnp.int32, (BF16_SUBLANES, seq), 0) == 0, 1.0, 0.0)
    ones_rows = ones_rows.astype(jnp.bfloat16)
    vt_ref[0] = jnp.concatenate([v_t[:HEAD_DIM], ones_rows], axis=0)
    vt_ref[1] = jnp.concatenate([v_t[HEAD_DIM:], ones_rows], axis=0)

    k_mean = jnp.sum(k_all.astype(jnp.float32).reshape(n_blocks, MOBA_BLOCK, LANES), axis=1)
    k_mean = k_mean * (1.0 / MOBA_BLOCK)
    km_a = jnp.broadcast_to(k_mean[:, None, :], (n_blocks, n_blocks, LANES)).reshape(n_pairs, LANES)
    km_b = jnp.broadcast_to(k_mean[None, :, :], (n_blocks, n_blocks, LANES)).reshape(n_pairs, LANES)
    pad = [jnp.zeros((HEAD_DIM - n_pairs, LANES), jnp.float32)] if n_pairs < HEAD_DIM else []

    def both_heads(km):
        return jnp.concatenate(
            [jnp.where(own0, km, 0.0)] + pad + [jnp.where(own0, 0.0, km)] + pad, axis=0)

    def split(m):
        hi = m.astype(jnp.bfloat16)
        return [hi, (m - hi.astype(jnp.float32)).astype(jnp.bfloat16)]
    gate_rhs = jnp.concatenate(split(both_heads(km_a)) + split(both_heads(km_b)), axis=0)

    trip = lane % HEAD_DIM
    trip_j, trip_jp = trip // n_blocks, trip % n_blocks
    trip_ok = trip < n_pairs
    r_idx = lax.broadcasted_iota(jnp.int32, (LANES, LANES), 0)
    c_idx = lax.broadcasted_iota(jnp.int32, (LANES, LANES), 1)
    r_bias_lane = jnp.where(r_idx < HEAD_DIM, HEAD_DIM, 0) + (r_idx % HEAD_DIM) // n_blocks
    rank_sum = jnp.where((c_idx == r_bias_lane) & (r_idx % HEAD_DIM < n_pairs), 1.0, 0.0)
    rank_sum = rank_sum.astype(jnp.bfloat16)
    bias_blk = jnp.where(own0, lane, lane - HEAD_DIM)
    bias_lane = bias_blk < n_blocks

    def gate_stage(i):
        q_i = q_ref[i * MOBA_BLOCK:(i + 1) * MOBA_BLOCK, :]
        return _dot_nt(q_i * jnp.bfloat16(HEAD_DIM ** -0.5), gate_rhs)

    def rank_stage(i, gates):
        gate_j = gates[:, 0:LANES] + gates[:, LANES:2 * LANES]
        gate_jp = gates[:, 2 * LANES:3 * LANES] + gates[:, 3 * LANES:]
        beats = (((gate_jp > gate_j) | ((gate_jp == gate_j) & (trip_jp < trip_j)))
                 & (trip_jp < i) & trip_ok)
        return _dot(jnp.where(beats, 1.0, 0.0).astype(jnp.bfloat16), rank_sum)

    def make_stages(overflow):
        single_pass = overflow is not None

        def score_stage(i, rank, out):
            keep = ((rank < MOBA_TOPK) & (bias_blk < i)) | (bias_blk == i) | ~bias_lane
            bias = jnp.where(keep, 0.0, MASK_NEG).astype(jnp.bfloat16)
            q_i = q_ref[i * MOBA_BLOCK:(i + 1) * MOBA_BLOCK, :]
            q_s = (q_i.astype(jnp.float32) * (HEAD_DIM ** -0.5 * LOG2_E)).astype(jnp.bfloat16)
            for hd, q_aug in enumerate((jnp.where(own0, q_s, bias), jnp.where(own0, bias, q_s))):
                if single_pass:
                    yield from _prob_ops(q_aug.T, ka_ref.at[hd], i, causal, out[hd])
                else:
                    yield from _score_ops(q_aug.T, ka_ref.at[hd], i, causal, out[hd])

        def value_stage(i, scores):
            out = []
            yield from _value_ops(scores[0], vt_ref.at[0], i, out, overflow)
            yield from _value_ops(scores[1], vt_ref.at[1], i, out, overflow)
            o_ref[i * MOBA_BLOCK:(i + 1) * MOBA_BLOCK, :] = (
                jnp.concatenate(out, axis=0).T.astype(o_ref.dtype))

        return _SlabStages(gate_stage, rank_stage, score_stage, value_stage)

    return make_stages


def _moba(q, k, v):
    n_batch, seq, d_attn = q.shape
    assert seq % MOBA_BLOCK == 0 and d_attn % LANES == 0
    n_blocks = seq // MOBA_BLOCK
    width = MOBA_SLABS * LANES
    assert d_attn % width == 0
    spec = pl.BlockSpec((None, seq, width), lambda b, s: (b, 0, s))
    return pl.pallas_call(
        functools.partial(_moba_kernel, n_blocks=n_blocks),
        out_shape=jax.ShapeDtypeStruct((n_batch, seq, d_attn), jnp.bfloat16),
        grid=(n_batch, d_attn // width),
        in_specs=[spec, spec, spec],
        out_specs=spec,
        scratch_shapes=[
            pltpu.VMEM((MOBA_SLABS, LANES // HEAD_DIM, seq, LANES), jnp.bfloat16),
            pltpu.VMEM((MOBA_SLABS, LANES // HEAD_DIM, HEAD_DIM + BF16_SUBLANES, seq),
                       jnp.bfloat16)],
        compiler_params=pltpu.CompilerParams(
            dimension_semantics=("arbitrary", "arbitrary"), vmem_limit_bytes=VMEM_LIMIT_BYTES),
        name="moba",
    )(q, k, v)


def _out_ffn_kernel(x_ref, ys_ref, ya_ref, gattn_ref, wo_ref, gpost_ref, gpre_ref,
                    wg_ref, wu_ref, wd_ref, gffn_ref, o_ref, acc_ref):
    n_chunks = wg_ref.shape[1] // FF_CHUNK
    half = x_ref.shape[0] // 2
    d_ssm = ys_ref.shape[1]

    def mix_stage(r):
        rows = slice(r * half, (r + 1) * half)
        ya = _rms(ya_ref[rows, :].astype(jnp.float32), gattn_ref[...]).astype(jnp.bfloat16)
        mixed = _dot(ys_ref[rows, :], wo_ref[:d_ssm, :]) + _dot(ya, wo_ref[d_ssm:, :])
        x1 = x_ref[rows, :] + _rms(mixed, gpost_ref[...])
        return x1, _rms(x1, gpre_ref[...]).astype(jnp.bfloat16)

    def gate_up(h, c):
        lo, hi = c * FF_CHUNK, (c + 1) * FF_CHUNK
        return _dot(h, wg_ref[:, lo:hi]), _dot(h, wu_ref[:, lo:hi])

    def ffn_chunk(r, h, c, pending):
        gate, up = pending
        pending = gate_up(h, c + 1) if c + 1 < n_chunks else None
        f = (jax.nn.silu(gate) * up).astype(jnp.bfloat16)
        part = _dot(f, wd_ref[c * FF_CHUNK:(c + 1) * FF_CHUNK, :])
        if c == 0:
            acc_ref[r] = part
        else:
            acc_ref[r] += part
        return pending

    def out_stage(r, x1):
        o_ref[r * half:(r + 1) * half, :] = x1 + _rms(acc_ref[r], gffn_ref[...])

    x1_a, h_a = mix_stage(0)
    x1_b, h_b = mix_stage(1)
    pending = gate_up(h_a, 0)
    for c in range(n_chunks):
        pending = ffn_chunk(0, h_a, c, pending)
    pending = gate_up(h_b, 0)
    for c in range(n_chunks):
        pending = ffn_chunk(1, h_b, c, pending)
        if c == 0:
            out_stage(0, x1_a)
    out_stage(1, x1_b)


def _out_ffn(x2, ys2, ya2, g_attn, wo, g_post, g_pre, wg, wu, wd, g_ffn):
    n_rows, d_model = x2.shape
    d_half = ys2.shape[1]
    d_ff = wg.shape[1]
    assert n_rows % ROW_TILE == 0 and d_ff % FF_CHUNK == 0
    row = lambda i: (i, 0)
    resident = lambda a: pl.BlockSpec(a.shape, lambda i: (0,) * a.ndim, pipeline_mode=pl.Buffered(1))
    consts = (g_attn, wo, g_post, g_pre, wg, wu, wd, g_ffn)
    return pl.pallas_call(
        _out_ffn_kernel,
        out_shape=jax.ShapeDtypeStruct((n_rows, d_model), jnp.float32),
        grid=(n_rows // ROW_TILE,),
        in_specs=[
            pl.BlockSpec((ROW_TILE, d_model), row),
            pl.BlockSpec((ROW_TILE, d_half), row),
            pl.BlockSpec((ROW_TILE, d_half), row),
        ] + [resident(a) for a in consts],
        out_specs=pl.BlockSpec((ROW_TILE, d_model), row),
        scratch_shapes=[pltpu.VMEM((2, ROW_TILE // 2, d_model), jnp.float32)],
        compiler_params=pltpu.CompilerParams(
            dimension_semantics=("arbitrary",), vmem_limit_bytes=VMEM_LIMIT_BYTES),
        name="out_ffn",
    )(x2, ys2, ya2, *consts)


def _s5_params(a_re, a_im, log_dt, b_re, b_im, c_re, c_im, n_batch):
    f32 = jnp.float32
    a_re, a_im = a_re.astype(f32), a_im.astype(f32)
    b_re, b_im = b_re.astype(f32), b_im.astype(f32)
    n_groups = a_re.shape[0]
    n_clusters = n_groups // GROUPS_PER_CLUSTER
    dt = jnp.exp(log_dt.astype(f32))[:, None]
    mag = jnp.exp(a_re * dt)
    ang = a_im * dt
    lb_re = mag * jnp.cos(ang)
    lb_im = mag * jnp.sin(ang)
    nr = lb_re - 1.0
    den = a_re * a_re + a_im * a_im
    cr = (nr * a_re + lb_im * a_im) / den
    ci = (lb_im * a_re - nr * a_im) / den
    bb_re = cr[..., None] * b_re - ci[..., None] * b_im
    bb_im = cr[..., None] * b_im + ci[..., None] * b_re
    eye = jnp.eye(GROUPS_PER_CLUSTER, dtype=f32)

    def in_proj(bb):
        bb = bb.reshape(n_clusters, GROUPS_PER_CLUSTER, SSM_STATE, SSM_GROUP)
        m = jnp.einsum('cgph,gk->cghkp', bb, eye)
        return m.reshape(n_clusters, LANES, CLUSTER_STATE)

    def out_proj(cc):
        cc = cc.reshape(n_clusters, GROUPS_PER_CLUSTER, SSM_GROUP, SSM_STATE)
        m = jnp.einsum('cghp,gk->cgpkh', cc, eye)
        return m.reshape(n_clusters, CLUSTER_STATE, LANES)

    def rotate(m_re, m_im, l_re, l_im):
        return m_re * l_re - m_im * l_im, m_re * l_im + m_im * l_re

    def state_cols(m_re, m_im):
        return jnp.concatenate([in_proj(m_re), in_proj(m_im)], axis=2)

    def state_rows(m_re, m_im):
        return jnp.concatenate([out_proj(m_re), -out_proj(m_im)], axis=1)

    bl_re, bl_im = rotate(bb_re, bb_im, lb_re[..., None], lb_im[..., None])
    wv = jnp.concatenate([state_cols(bl_re, bl_im), state_cols(bb_re, bb_im)], axis=1)
    c_re, c_im = c_re.astype(f32), c_im.astype(f32)
    cl_re, cl_im = rotate(c_re, c_im, lb_re[:, None, :], lb_im[:, None, :])
    cz = jnp.concatenate([state_rows(c_re, c_im), state_rows(cl_re, cl_im)], axis=2)
    direct = (jnp.einsum('gop,gpi->gio', c_re, bb_re) - jnp.einsum('gop,gpi->gio', c_im, bb_im))
    direct = direct.reshape(n_clusters, GROUPS_PER_CLUSTER, SSM_GROUP, SSM_GROUP)
    dmat = jnp.einsum('cgio,gk->cgiko', direct, eye).reshape(n_clusters, LANES, LANES)
    l2_re, l2_im = rotate(lb_re, lb_im, lb_re, lb_im)
    lam2_re = jnp.broadcast_to(l2_re.reshape(1, -1), (n_batch, l2_re.size))
    lam2_im = jnp.broadcast_to(l2_im.reshape(1, -1), (n_batch, l2_im.size))
    bf16 = jnp.bfloat16
    return lam2_re, lam2_im, wv.astype(bf16), cz.astype(bf16), dmat.astype(bf16)


def _time_batch_permutation(n_batch):
    n = n_batch * PERM_T
    r = jnp.arange(n)
    parity, rest = r // (n // 2), r % (n // 2)
    src = (rest % n_batch) * PERM_T + 2 * (rest // n_batch) + parity
    return (src[:, None] == jnp.arange(n)[None, :]).astype(jnp.bfloat16)


def _layer(x, g_pre_mix, w_in, ssm_a_re, ssm_a_im, ssm_log_dt, ssm_b_re, ssm_b_im, ssm_c_re,
           ssm_c_im, ssm_d, w_glu, b_glu, g_ssm_out, g_attn_out, w_out, g_post_mix, g_pre_ffn,
           w_gate, w_up, w_down, g_post_ffn):
    n_batch, seq, d_model = x.shape
    d_ssm = w_glu.shape[0]
    bf16 = jnp.bfloat16
    row = lambda g: g.reshape(1, -1).astype(jnp.float32)
    lam_re, lam_im, wv, cz, dmat = _s5_params(ssm_a_re, ssm_a_im, ssm_log_dt, ssm_b_re, ssm_b_im,
                                              ssm_c_re, ssm_c_im, n_batch)
    perm = _time_batch_permutation(n_batch)
    q, k, v, y_ssm = _mixer_in(
        x, row(g_pre_mix), w_in, perm, perm.T, lam_re, lam_im, wv, cz, dmat, row(ssm_d), w_glu,
        row(b_glu), row(g_ssm_out))
    y_attn = _moba(q, k, v)
    n_rows = n_batch * seq
    out = _out_ffn(
        x.reshape(n_rows, d_model), y_ssm.reshape(n_rows, d_ssm), y_attn.reshape(n_rows, -1),
        row(g_attn_out), w_out.astype(bf16), row(g_post_mix), row(g_pre_ffn), w_gate.astype(bf16),
        w_up.astype(bf16), w_down.astype(bf16), row(g_post_ffn))
    return out.reshape(n_batch, seq, d_model)


def kernel(x, g_pre_mix, w_in, ssm_a_re, ssm_a_im, ssm_log_dt, ssm_b_re, ssm_b_im, ssm_c_re, ssm_c_im, ssm_d, w_glu, b_glu, g_ssm_out, g_attn_out, w_out, g_post_mix, g_pre_ffn, w_gate, w_up, w_down, g_post_ffn):
    for l in range(w_in.shape[0]):
        x = _layer(x, g_pre_mix[l], w_in[l], ssm_a_re[l], ssm_a_im[l], ssm_log_dt[l], ssm_b_re[l],
                   ssm_b_im[l], ssm_c_re[l], ssm_c_im[l], ssm_d[l], w_glu[l], b_glu[l],
                   g_ssm_out[l], g_attn_out[l], w_out[l], g_post_mix[l], g_pre_ffn[l], w_gate[l],
                   w_up[l], w_down[l], g_post_ffn[l])
    return x
```

```python
import collections
import functools
import math

import jax
import jax.numpy as jnp
from jax import lax
from jax.experimental import pallas as pl
from jax.experimental.pallas import tpu as pltpu

SSM_GROUP = 16
SSM_STATE = 64
HEAD_DIM = 64
MOBA_BLOCK = 256
MOBA_TOPK = 3
RMS_EPS = 1e-6

LANES = 128
BF16_SUBLANES = 16
VMEM_LIMIT_BYTES = 56 * 1024 * 1024

GROUPS_PER_CLUSTER = LANES // SSM_GROUP
CLUSTER_STATE = GROUPS_PER_CLUSTER * SSM_STATE
TIME_TILE = 64
PERM_T = BF16_SUBLANES
MOBA_SLABS = 2
ROW_TILE = 1024
ROW_PARTS = 4
FF_CHUNK = 256
MASK_NEG = -1e30
LOG2_E = math.log2(math.e)

_NT = (((1,), (1,)), ((), ()))
_SlabStages = collections.namedtuple('_SlabStages', 'gate_stage rank_stage score_stage value_stage')


def _rms(x, g):
    return x * lax.rsqrt(jnp.mean(x * x, axis=-1, keepdims=True) + RMS_EPS) * g


def _dot(a, b):
    return jnp.dot(a, b, preferred_element_type=jnp.float32)


def _dot_nt(a, b):
    return lax.dot_general(a, b, _NT, preferred_element_type=jnp.float32)


def _mixer_in_kernel(x_ref, gpre_ref, win_ref, perm_ref, permt_ref, lamre_ref, lamim_ref, wv_ref,
                     cz_ref, dmat_ref, dskip_ref, wglu_ref, bglu_ref, gssm_ref,
                     q_ref, k_ref, v_ref, yssm_ref,
                     state_ref, utm_ref, bu_ref, s_ref, y_ref, *, n_batch, n_clusters):
    lt = x_ref.shape[1]
    d_model = x_ref.shape[2]
    d_ssm = utm_ref.shape[2]
    rows = n_batch * lt
    half = rows // 2
    perm_rows = n_batch * PERM_T

    @pl.when(pl.program_id(0) == 0)
    def _():
        state_ref[...] = jnp.zeros_like(state_ref)

    x = x_ref[...].reshape(rows, d_model)
    h = _rms(x, gpre_ref[...]).astype(jnp.bfloat16)
    u = _dot(h, win_ref[:, 0:d_ssm]).astype(jnp.bfloat16)

    proj_cols = 2 * LANES
    projections = [(ref, col, tile) for ref, col in ((q_ref, 1), (k_ref, 2), (v_ref, 3))
                   for tile in range(d_ssm // proj_cols)]

    def project(count):
        for _ in range(min(count, len(projections))):
            ref, col, tile = projections.pop(0)
            c0 = col * d_ssm + tile * proj_cols
            z = _dot(h, win_ref[:, c0:c0 + proj_cols]).astype(jnp.bfloat16)
            ref[:, :, tile * proj_cols:(tile + 1) * proj_cols] = z.reshape(n_batch, lt, proj_cols)

    project(1)
    perm = perm_ref[...]
    pair_rows = perm_rows // 2
    for th in range(lt // PERM_T):
        chunk = jnp.concatenate(
            [u[b * lt + th * PERM_T: b * lt + (th + 1) * PERM_T] for b in range(n_batch)], axis=0)
        sorted_rows = _dot(perm, chunk).astype(jnp.bfloat16)
        for parity in range(2):
            utm_ref[parity, th * pair_rows:(th + 1) * pair_rows, :] = (
                sorted_rows[parity * pair_rows:(parity + 1) * pair_rows])
    for c in range(n_clusters):
        cols = slice(c * LANES, (c + 1) * LANES)
        bu_ref[c] = _dot(jnp.concatenate([utm_ref[0, :, cols], utm_ref[1, :, cols]], axis=1),
                         wv_ref[c])
        project(1)

    for c in range(n_clusters):
        lo, hi = c * CLUSTER_STATE, (c + 1) * CLUSTER_STATE
        cols = slice(c * LANES, (c + 1) * LANES)
        lam_re = lamre_ref[:, lo:hi]
        lam_im = lamim_ref[:, lo:hi]
        h_re = state_ref[0, :, lo:hi]
        h_im = state_ref[1, :, lo:hi]
        s_ref[c, 0:n_batch, 0:CLUSTER_STATE] = h_re.astype(jnp.bfloat16)
        s_ref[c, 0:n_batch, CLUSTER_STATE:2 * CLUSTER_STATE] = h_im.astype(jnp.bfloat16)
        for j in range(lt // 2):
            r0 = j * n_batch
            b_re = bu_ref[c, r0:r0 + n_batch, 0:CLUSTER_STATE]
            b_im = bu_ref[c, r0:r0 + n_batch, CLUSTER_STATE:2 * CLUSTER_STATE]
            n_re = lam_re * h_re - lam_im * h_im + b_re
            n_im = lam_re * h_im + lam_im * h_re + b_im
            h_re, h_im = n_re, n_im
            r1 = r0 + n_batch
            s_ref[c, r1:r1 + n_batch, 0:CLUSTER_STATE] = n_re.astype(jnp.bfloat16)
            s_ref[c, r1:r1 + n_batch, CLUSTER_STATE:2 * CLUSTER_STATE] = n_im.astype(jnp.bfloat16)
        state_ref[0, :, lo:hi] = h_re
        state_ref[1, :, lo:hi] = h_im
        z = _dot(s_ref[c], cz_ref[c])
        y_ref[0, :, cols] = z[:half, LANES:] + _dot(utm_ref[0, :, cols], dmat_ref[c])
        y_ref[1, :, cols] = z[n_batch:, :LANES]

    perm_t = permt_ref[...]
    n_groups = lt // PERM_T
    half_groups = n_groups // 2

    def glu_stage(g0):
        span = slice(g0 * pair_rows, (g0 + half_groups) * pair_rows)
        y = jnp.concatenate(
            [y_ref[parity, span, :] + dskip_ref[...] * utm_ref[parity, span, :].astype(jnp.float32)
             for parity in range(2)], axis=0)
        y = jax.nn.gelu(y)
        return y, _dot(y.astype(jnp.bfloat16), wglu_ref[...])

    def store_stage(g0, y, glu):
        y = y * jax.nn.sigmoid(glu + bglu_ref[...])
        yn = _rms(y, gssm_ref[...]).astype(jnp.bfloat16)
        quarter = half_groups * pair_rows
        for g in range(half_groups):
            sorted_rows = jnp.concatenate(
                [yn[parity * quarter + g * pair_rows: parity * quarter + (g + 1) * pair_rows]
                 for parity in range(2)], axis=0)
            back = _dot(perm_t, sorted_rows).astype(jnp.bfloat16)
            th = g0 + g
            for b in range(n_batch):
                yssm_ref[b, th * PERM_T:(th + 1) * PERM_T, :] = back[b * PERM_T:(b + 1) * PERM_T]

    first = glu_stage(0)
    project(len(projections))
    second = glu_stage(half_groups)
    store_stage(0, *first)
    store_stage(half_groups, *second)


def _mixer_in(x, g_pre, w_in, perm, perm_t, lam_re, lam_im, wv, cz, dmat, d_skip, w_glu, b_glu,
              g_ssm):
    n_batch, seq, d_model = x.shape
    d_ssm = w_glu.shape[0]
    n_clusters = d_ssm // LANES
    lt = TIME_TILE
    rows = n_batch * lt
    assert seq % lt == 0 and lt % PERM_T == 0 and n_batch % BF16_SUBLANES == 0
    resident = lambda a: pl.BlockSpec(a.shape, lambda i: (0,) * a.ndim, pipeline_mode=pl.Buffered(1))
    act_spec = pl.BlockSpec((n_batch, lt, d_ssm), lambda i: (0, i, 0))
    act_shape = jax.ShapeDtypeStruct((n_batch, seq, d_ssm), jnp.bfloat16)
    consts = (g_pre, w_in, perm, perm_t, lam_re, lam_im, wv, cz, dmat, d_skip, w_glu, b_glu, g_ssm)
    return pl.pallas_call(
        functools.partial(_mixer_in_kernel, n_batch=n_batch, n_clusters=n_clusters),
        out_shape=(act_shape,) * 4,
        grid=(seq // lt,),
        in_specs=[pl.BlockSpec((n_batch, lt, d_model), lambda i: (0, i, 0))]
        + [resident(a) for a in consts],
        out_specs=(act_spec,) * 4,
        scratch_shapes=[
            pltpu.VMEM((2, n_batch, n_clusters * CLUSTER_STATE), jnp.float32),
            pltpu.VMEM((2, rows // 2, d_ssm), jnp.bfloat16),
            pltpu.VMEM((n_clusters, rows // 2, 2 * CLUSTER_STATE), jnp.float32),
            pltpu.VMEM((n_clusters, rows // 2 + n_batch, 2 * CLUSTER_STATE), jnp.bfloat16),
            pltpu.VMEM((2, rows // 2, d_ssm), jnp.float32),
        ],
        compiler_params=pltpu.CompilerParams(
            dimension_semantics=("arbitrary",), vmem_limit_bytes=VMEM_LIMIT_BYTES),
        name="mixer_in",
    )(x, *consts)


def _block_order(i):
    return [i] + list(range(i))


def _score_ops(q_aug_t, ka_ref, i, causal, out):
    for j in _block_order(i):
        s = _dot(ka_ref[j * MOBA_BLOCK:(j + 1) * MOBA_BLOCK, :], q_aug_t)
        out.append(jnp.where(causal, s, MASK_NEG) if j == i else s)
        yield


def _prob_ops(q_aug_t, ka_ref, i, causal, out):
    m_own = None
    for j in _block_order(i):
        s = _dot(ka_ref[j * MOBA_BLOCK:(j + 1) * MOBA_BLOCK, :], q_aug_t)
        if j == i:
            s = jnp.where(causal, s, MASK_NEG)
            m_own = jnp.max(s, axis=0, keepdims=True)
        out.append(jnp.exp2(s - m_own).astype(jnp.bfloat16))
        yield


def _value_ops(scores, vt_ref, i, out, overflow):
    normalised = overflow is not None
    if not normalised:
        m = functools.reduce(jnp.maximum, [jnp.max(s, axis=0, keepdims=True) for s in scores])
    acc = None
    for j, s in zip(_block_order(i), scores):
        p = s if normalised else jnp.exp2(s - m).astype(jnp.bfloat16)
        part = _dot(vt_ref[:, j * MOBA_BLOCK:(j + 1) * MOBA_BLOCK], p)
        acc = part if acc is None else acc + part
        yield
    if normalised:
        overflow.append(jnp.where(jnp.isfinite(acc), 0.0, 1.0))
    out.append(acc[:HEAD_DIM] / acc[HEAD_DIM:HEAD_DIM + 1])


def _interleave(*generators):
    live = list(generators)
    while live:
        for g in list(live):
            try:
                next(g)
            except StopIteration:
                live.remove(g)


def _moba_kernel(q_ref, k_ref, v_ref, o_ref, ka_ref, vt_ref, *, n_blocks):
    slabs = []
    for s in range(q_ref.shape[1] // LANES):
        cols = slice(s * LANES, (s + 1) * LANES)
        slabs.append(_moba_slab(q_ref.at[:, cols], k_ref.at[:, cols], v_ref.at[:, cols],
                                o_ref.at[:, cols], ka_ref.at[s], vt_ref.at[s], n_blocks))

    overflow = []
    _run_stages([slab(overflow) for slab in slabs], n_blocks)
    overflowed = jnp.max(functools.reduce(jnp.maximum, overflow)) > 0.0

    @pl.when(overflowed)
    def _():
        _run_stages([slab(None) for slab in slabs], n_blocks)


def _run_stages(stages, n_blocks):
    items = [(s, i) for s in range(len(stages)) for i in range(n_blocks - 1, -1, -1)]
    gates, ranks, scores = {}, {}, {}
    for step in range(len(items) + 3):
        if step < len(items):
            s, i = items[step]
            gates[s, i] = stages[s].gate_stage(i)
        if 0 <= step - 1 < len(items):
            s, i = items[step - 1]
            ranks[s, i] = stages[s].rank_stage(i, gates.pop((s, i)))
        pending = []
        if 0 <= step - 2 < len(items):
            s, i = items[step - 2]
            scores[s, i] = ([], [])
            pending.append(stages[s].score_stage(i, ranks.pop((s, i)), scores[s, i]))
        if 0 <= step - 3 < len(items):
            s, i = items[step - 3]
            pending.append(stages[s].value_stage(i, scores.pop((s, i))))
        _interleave(*pending)


def _moba_slab(q_ref, k_ref, v_ref, o_ref, ka_ref, vt_ref, n_blocks):
    seq = q_ref.shape[0]
    n_pairs = n_blocks * n_blocks
    assert LANES == 2 * HEAD_DIM and n_pairs <= HEAD_DIM
    lane = lax.broadcasted_iota(jnp.int32, (1, LANES), 1)
    own0 = lane < HEAD_DIM
    causal = (lax.broadcasted_iota(jnp.int32, (MOBA_BLOCK, MOBA_BLOCK), 0)
              <= lax.broadcasted_iota(jnp.int32, (MOBA_BLOCK, MOBA_BLOCK), 1))

    key_blk = lax.broadcasted_iota(jnp.int32, (seq, 1), 0) // MOBA_BLOCK
    lane_full = lax.broadcasted_iota(jnp.int32, (seq, LANES), 1)
    k_all = k_ref[...]
    ka_ref[0] = jnp.where(own0, k_all,
                          jnp.where(lane_full == key_blk + HEAD_DIM, 1.0, 0.0).astype(jnp.bfloat16))
    ka_ref[1] = jnp.where(own0, jnp.where(lane_full == key_blk, 1.0, 0.0).astype(jnp.bfloat16), k_all)
    v_t = v_ref[...].T
    ones_rows = jnp.where(lax.broadcasted_iota(jnp.int32, (BF16_SUBLANES, seq), 0) == 0, 1.0, 0.0)
    ones_rows = ones_rows.astype(jnp.bfloat16)
    vt_ref[0] = jnp.concatenate([v_t[:HEAD_DIM], ones_rows], axis=0)
    vt_ref[1] = jnp.concatenate([v_t[HEAD_DIM:], ones_rows], axis=0)

    k_mean = jnp.sum(k_all.astype(jnp.float32).reshape(n_blocks, MOBA_BLOCK, LANES), axis=1)
    k_mean = k_mean * (1.0 / MOBA_BLOCK)
    km_a = jnp.broadcast_to(k_mean[:, None, :], (n_blocks, n_blocks, LANES)).reshape(n_pairs, LANES)
    km_b = jnp.broadcast_to(k_mean[None, :, :], (n_blocks, n_blocks, LANES)).reshape(n_pairs, LANES)
    pad = [jnp.zeros((HEAD_DIM - n_pairs, LANES), jnp.float32)] if n_pairs < HEAD_DIM else []

    def both_heads(km):
        return jnp.concatenate(
            [jnp.where(own0, km, 0.0)] + pad + [jnp.where(own0, 0.0, km)] + pad, axis=0)

    def split(m):
        hi = m.astype(jnp.bfloat16)
        return [hi, (m - hi.astype(jnp.float32)).astype(jnp.bfloat16)]
    gate_rhs = jnp.concatenate(split(both_heads(km_a)) + split(both_heads(km_b)), axis=0)

    trip = lane % HEAD_DIM
    trip_j, trip_jp = trip // n_blocks, trip % n_blocks
    trip_ok = trip < n_pairs
    r_idx = lax.broadcasted_iota(jnp.int32, (LANES, LANES), 0)
    c_idx = lax.broadcasted_iota(jnp.int32, (LANES, LANES), 1)
    r_bias_lane = jnp.where(r_idx < HEAD_DIM, HEAD_DIM, 0) + (r_idx % HEAD_DIM) // n_blocks
    rank_sum = jnp.where((c_idx == r_bias_lane) & (r_idx % HEAD_DIM < n_pairs), 1.0, 0.0)
    rank_sum = rank_sum.astype(jnp.bfloat16)
    bias_blk = jnp.where(own0, lane, lane - HEAD_DIM)
    bias_lane = bias_blk < n_blocks

    def gate_stage(i):
        q_i = q_ref[i * MOBA_BLOCK:(i + 1) * MOBA_BLOCK, :]
        return _dot_nt(q_i * jnp.bfloat16(HEAD_DIM ** -0.5), gate_rhs)

    def rank_stage(i, gates):
        gate_j = gates[:, 0:LANES] + gates[:, LANES:2 * LANES]
        gate_jp = gates[:, 2 * LANES:3 * LANES] + gates[:, 3 * LANES:]
        beats = (((gate_jp > gate_j) | ((gate_jp == gate_j) & (trip_jp < trip_j)))
                 & (trip_jp < i) & trip_ok)
        return _dot(jnp.where(beats, 1.0, 0.0).astype(jnp.bfloat16), rank_sum)

    def make_stages(overflow):
        single_pass = overflow is not None

        def score_stage(i, rank, out):
            keep = ((rank < MOBA_TOPK) & (bias_blk < i)) | (bias_blk == i) | ~bias_lane
            bias = jnp.where(keep, 0.0, MASK_NEG).astype(jnp.bfloat16)
            q_i = q_ref[i * MOBA_BLOCK:(i + 1) * MOBA_BLOCK, :]
            q_s = (q_i.astype(jnp.float32) * (HEAD_DIM ** -0.5 * LOG2_E)).astype(jnp.bfloat16)
            for hd, q_aug in enumerate((jnp.where(own0, q_s, bias), jnp.where(own0, bias, q_s))):
                if single_pass:
                    yield from _prob_ops(q_aug.T, ka_ref.at[hd], i, causal, out[hd])
                else:
                    yield from _score_ops(q_aug.T, ka_ref.at[hd], i, causal, out[hd])

        def value_stage(i, scores):
            out = []
            yield from _value_ops(scores[0], vt_ref.at[0], i, out, overflow)
            yield from _value_ops(scores[1], vt_ref.at[1], i, out, overflow)
            o_ref[i * MOBA_BLOCK:(i + 1) * MOBA_BLOCK, :] = (
                jnp.concatenate(out, axis=0).T.astype(o_ref.dtype))

        return _SlabStages(gate_stage, rank_stage, score_stage, value_stage)

    return make_stages


def _moba(q, k, v):
    n_batch, seq, d_attn = q.shape
    assert seq % MOBA_BLOCK == 0 and d_attn % LANES == 0
    n_blocks = seq // MOBA_BLOCK
    width = MOBA_SLABS * LANES
    assert d_attn % width == 0
    spec = pl.BlockSpec((None, seq, width), lambda b, s: (b, 0, s))
    return pl.pallas_call(
        functools.partial(_moba_kernel, n_blocks=n_blocks),
        out_shape=jax.ShapeDtypeStruct((n_batch, seq, d_attn), jnp.bfloat16),
        grid=(n_batch, d_attn // width),
        in_specs=[spec, spec, spec],
        out_specs=spec,
        scratch_shapes=[
            pltpu.VMEM((MOBA_SLABS, LANES // HEAD_DIM, seq, LANES), jnp.bfloat16),
            pltpu.VMEM((MOBA_SLABS, LANES // HEAD_DIM, HEAD_DIM + BF16_SUBLANES, seq),
                       jnp.bfloat16)],
        compiler_params=pltpu.CompilerParams(
            dimension_semantics=("arbitrary", "arbitrary"), vmem_limit_bytes=VMEM_LIMIT_BYTES),
        name="moba",
    )(q, k, v)


def _out_ffn_kernel(x_ref, ys_ref, ya_ref, gattn_ref, wo_ref, gpost_ref, gpre_ref,
                    wg_ref, wu_ref, wd_ref, gffn_ref, o_ref, acc_ref):
    n_chunks = wg_ref.shape[1] // FF_CHUNK
    part_rows = x_ref.shape[0] // ROW_PARTS
    d_ssm = ys_ref.shape[1]

    def mix_stage(r):
        rows = slice(r * part_rows, (r + 1) * part_rows)
        ya = _rms(ya_ref[rows, :].astype(jnp.float32), gattn_ref[...]).astype(jnp.bfloat16)
        mixed = _dot(ys_ref[rows, :], wo_ref[:d_ssm, :]) + _dot(ya, wo_ref[d_ssm:, :])
        x1 = x_ref[rows, :] + _rms(mixed, gpost_ref[...])
        return x1, _rms(x1, gpre_ref[...]).astype(jnp.bfloat16)

    def gate_up(h, c):
        lo, hi = c * FF_CHUNK, (c + 1) * FF_CHUNK
        return _dot(h, wg_ref[:, lo:hi]), _dot(h, wu_ref[:, lo:hi])

    def ffn_chunk(r, h, c, pending):
        gate, up = pending
        pending = gate_up(h, c + 1) if c + 1 < n_chunks else None
        f = (jax.nn.silu(gate) * up).astype(jnp.bfloat16)
        part = _dot(f, wd_ref[c * FF_CHUNK:(c + 1) * FF_CHUNK, :])
        if c == 0:
            acc_ref[r] = part
        else:
            acc_ref[r] += part
        return pending

    def out_stage(r, x1):
        o_ref[r * part_rows:(r + 1) * part_rows, :] = x1 + _rms(acc_ref[r], gffn_ref[...])

    parts = [mix_stage(r) for r in range(ROW_PARTS)]
    for r, (_, h) in enumerate(parts):
        pending = gate_up(h, 0)
        for c in range(n_chunks):
            pending = ffn_chunk(r, h, c, pending)
            if c == 0 and r > 0:
                out_stage(r - 1, parts[r - 1][0])
    out_stage(ROW_PARTS - 1, parts[-1][0])


def _out_ffn(x2, ys2, ya2, g_attn, wo, g_post, g_pre, wg, wu, wd, g_ffn):
    n_rows, d_model = x2.shape
    d_half = ys2.shape[1]
    d_ff = wg.shape[1]
    assert n_rows % ROW_TILE == 0 and d_ff % FF_CHUNK == 0
    row = lambda i: (i, 0)
    resident = lambda a: pl.BlockSpec(a.shape, lambda i: (0,) * a.ndim, pipeline_mode=pl.Buffered(1))
    consts = (g_attn, wo, g_post, g_pre, wg, wu, wd, g_ffn)
    return pl.pallas_call(
        _out_ffn_kernel,
        out_shape=jax.ShapeDtypeStruct((n_rows, d_model), jnp.float32),
        grid=(n_rows // ROW_TILE,),
        in_specs=[
            pl.BlockSpec((ROW_TILE, d_model), row),
            pl.BlockSpec((ROW_TILE, d_half), row),
            pl.BlockSpec((ROW_TILE, d_half), row),
        ] + [resident(a) for a in consts],
        out_specs=pl.BlockSpec((ROW_TILE, d_model), row),
        scratch_shapes=[pltpu.VMEM((ROW_PARTS, ROW_TILE // ROW_PARTS, d_model), jnp.float32)],
        compiler_params=pltpu.CompilerParams(
            dimension_semantics=("arbitrary",), vmem_limit_bytes=VMEM_LIMIT_BYTES),
        name="out_ffn",
    )(x2, ys2, ya2, *consts)


def _s5_params(a_re, a_im, log_dt, b_re, b_im, c_re, c_im, n_batch):
    f32 = jnp.float32
    a_re, a_im = a_re.astype(f32), a_im.astype(f32)
    b_re, b_im = b_re.astype(f32), b_im.astype(f32)
    n_groups = a_re.shape[0]
    n_clusters = n_groups // GROUPS_PER_CLUSTER
    dt = jnp.exp(log_dt.astype(f32))[:, None]
    mag = jnp.exp(a_re * dt)
    ang = a_im * dt
    lb_re = mag * jnp.cos(ang)
    lb_im = mag * jnp.sin(ang)
    nr = lb_re - 1.0
    den = a_re * a_re + a_im * a_im
    cr = (nr * a_re + lb_im * a_im) / den
    ci = (lb_im * a_re - nr * a_im) / den
    bb_re = cr[..., None] * b_re - ci[..., None] * b_im
    bb_im = cr[..., None] * b_im + ci[..., None] * b_re
    eye = jnp.eye(GROUPS_PER_CLUSTER, dtype=f32)

    def in_proj(bb):
        bb = bb.reshape(n_clusters, GROUPS_PER_CLUSTER, SSM_STATE, SSM_GROUP)
        m = jnp.einsum('cgph,gk->cghkp', bb, eye)
        return m.reshape(n_clusters, LANES, CLUSTER_STATE)

    def out_proj(cc):
        cc = cc.reshape(n_clusters, GROUPS_PER_CLUSTER, SSM_GROUP, SSM_STATE)
        m = jnp.einsum('cghp,gk->cgpkh', cc, eye)
        return m.reshape(n_clusters, CLUSTER_STATE, LANES)

    def rotate(m_re, m_im, l_re, l_im):
        return m_re * l_re - m_im * l_im, m_re * l_im + m_im * l_re

    def state_cols(m_re, m_im):
        return jnp.concatenate([in_proj(m_re), in_proj(m_im)], axis=2)

    def state_rows(m_re, m_im):
        return jnp.concatenate([out_proj(m_re), -out_proj(m_im)], axis=1)

    bl_re, bl_im = rotate(bb_re, bb_im, lb_re[..., None], lb_im[..., None])
    wv = jnp.concatenate([state_cols(bl_re, bl_im), state_cols(bb_re, bb_im)], axis=1)
    c_re, c_im = c_re.astype(f32), c_im.astype(f32)
    cl_re, cl_im = rotate(c_re, c_im, lb_re[:, None, :], lb_im[:, None, :])
    cz = jnp.concatenate([state_rows(c_re, c_im), state_rows(cl_re, cl_im)], axis=2)
    direct = (jnp.einsum('gop,gpi->gio', c_re, bb_re) - jnp.einsum('gop,gpi->gio', c_im, bb_im))
    direct = direct.reshape(n_clusters, GROUPS_PER_CLUSTER, SSM_GROUP, SSM_GROUP)
    dmat = jnp.einsum('cgio,gk->cgiko', direct, eye).reshape(n_clusters, LANES, LANES)
    l2_re, l2_im = rotate(lb_re, lb_im, lb_re, lb_im)
    lam2_re = jnp.broadcast_to(l2_re.reshape(1, -1), (n_batch, l2_re.size))
    lam2_im = jnp.broadcast_to(l2_im.reshape(1, -1), (n_batch, l2_im.size))
    bf16 = jnp.bfloat16
    return lam2_re, lam2_im, wv.astype(bf16), cz.astype(bf16), dmat.astype(bf16)


def _time_batch_permutation(n_batch):
    n = n_batch * PERM_T
    r = jnp.arange(n)
    parity, rest = r // (n // 2), r % (n // 2)
    src = (rest % n_batch) * PERM_T + 2 * (rest // n_batch) + parity
    return (src[:, None] == jnp.arange(n)[None, :]).astype(jnp.bfloat16)


def _layer(x, g_pre_mix, w_in, ssm_a_re, ssm_a_im, ssm_log_dt, ssm_b_re, ssm_b_im, ssm_c_re,
           ssm_c_im, ssm_d, w_glu, b_glu, g_ssm_out, g_attn_out, w_out, g_post_mix, g_pre_ffn,
           w_gate, w_up, w_down, g_post_ffn):
    n_batch, seq, d_model = x.shape
    d_ssm = w_glu.shape[0]
    bf16 = jnp.bfloat16
    row = lambda g: g.reshape(1, -1).astype(jnp.float32)
    lam_re, lam_im, wv, cz, dmat = _s5_params(ssm_a_re, ssm_a_im, ssm_log_dt, ssm_b_re, ssm_b_im,
                                              ssm_c_re, ssm_c_im, n_batch)
    perm = _time_batch_permutation(n_batch)
    q, k, v, y_ssm = _mixer_in(
        x, row(g_pre_mix), w_in, perm, perm.T, lam_re, lam_im, wv, cz, dmat, row(ssm_d), w_glu,
        row(b_glu), row(g_ssm_out))
    y_attn = _moba(q, k, v)
    n_rows = n_batch * seq
    out = _out_ffn(
        x.reshape(n_rows, d_model), y_ssm.reshape(n_rows, d_ssm), y_attn.reshape(n_rows, -1),
        row(g_attn_out), w_out.astype(bf16), row(g_post_mix), row(g_pre_ffn), w_gate.astype(bf16),
        w_up.astype(bf16), w_down.astype(bf16), row(g_post_ffn))
    return out.reshape(n_batch, seq, d_model)


def kernel(x, g_pre_mix, w_in, ssm_a_re, ssm_a_im, ssm_log_dt, ssm_b_re, ssm_b_im, ssm_c_re, ssm_c_im, ssm_d, w_glu, b_glu, g_ssm_out, g_attn_out, w_out, g_post_mix, g_pre_ffn, w_gate, w_up, w_down, g_post_ffn):
    for l in range(w_in.shape[0]):
        x = _layer(x, g_pre_mix[l], w_in[l], ssm_a_re[l], ssm_a_im[l], ssm_log_dt[l], ssm_b_re[l],
                   ssm_b_im[l], ssm_c_re[l], ssm_c_im[l], ssm_d[l], w_glu[l], b_glu[l],
                   g_ssm_out[l], g_attn_out[l], w_out[l], g_post_mix[l], g_pre_ffn[l], w_gate[l],
                   w_up[l], w_down[l], g_post_ffn[l])
    return x
```

```python
import collections
import functools
import math

import jax
import jax.numpy as jnp
from jax import lax
from jax.experimental import pallas as pl
from jax.experimental.pallas import tpu as pltpu

SSM_GROUP = 16
SSM_STATE = 64
HEAD_DIM = 64
MOBA_BLOCK = 256
MOBA_TOPK = 3
RMS_EPS = 1e-6

LANES = 128
BF16_SUBLANES = 16
VMEM_LIMIT_BYTES = 56 * 1024 * 1024

GROUPS_PER_CLUSTER = LANES // SSM_GROUP
CLUSTER_STATE = GROUPS_PER_CLUSTER * SSM_STATE
TIME_TILE = 64
PERM_T = BF16_SUBLANES
MOBA_SLABS = 2
ROW_TILE = 1024
ROW_PARTS = 4
FF_CHUNK = 256
MASK_NEG = -1e30
LOG2_E = math.log2(math.e)

_NT = (((1,), (1,)), ((), ()))
_SlabStages = collections.namedtuple('_SlabStages', 'gate_stage rank_stage score_stage value_stage')


def _rms(x, g):
    return x * lax.rsqrt(jnp.mean(x * x, axis=-1, keepdims=True) + RMS_EPS) * g


def _dot(a, b):
    return jnp.dot(a, b, preferred_element_type=jnp.float32)


def _dot_nt(a, b):
    return lax.dot_general(a, b, _NT, preferred_element_type=jnp.float32)


def _mixer_in_kernel(x_ref, gpre_ref, win_ref, perm_ref, permt_ref, lamre_ref, lamim_ref, wv_ref,
                     cz_ref, dmat_ref, dskip_ref, wglu_ref, bglu_ref, gssm_ref,
                     q_ref, k_ref, v_ref, yssm_ref,
                     state_ref, utm_ref, bu_ref, s_ref, y_ref, *, n_batch, n_clusters):
    lt = x_ref.shape[1]
    d_model = x_ref.shape[2]
    d_ssm = utm_ref.shape[2]
    rows = n_batch * lt
    half = rows // 2
    perm_rows = n_batch * PERM_T

    @pl.when(pl.program_id(0) == 0)
    def _():
        state_ref[...] = jnp.zeros_like(state_ref)

    x = x_ref[...].reshape(rows, d_model)
    h = _rms(x, gpre_ref[...]).astype(jnp.bfloat16)
    u = _dot(h, win_ref[:, 0:d_ssm]).astype(jnp.bfloat16)

    proj_cols = 2 * LANES
    projections = [(ref, col, tile) for ref, col in ((q_ref, 1), (k_ref, 2), (v_ref, 3))
                   for tile in range(d_ssm // proj_cols)]

    def project(count):
        for _ in range(min(count, len(projections))):
            ref, col, tile = projections.pop(0)
            c0 = col * d_ssm + tile * proj_cols
            z = _dot(h, win_ref[:, c0:c0 + proj_cols]).astype(jnp.bfloat16)
            ref[:, :, tile * proj_cols:(tile + 1) * proj_cols] = z.reshape(n_batch, lt, proj_cols)

    project(1)
    perm = perm_ref[...]
    pair_rows = perm_rows // 2
    for th in range(lt // PERM_T):
        chunk = jnp.concatenate(
            [u[b * lt + th * PERM_T: b * lt + (th + 1) * PERM_T] for b in range(n_batch)], axis=0)
        sorted_rows = _dot(perm, chunk).astype(jnp.bfloat16)
        for parity in range(2):
            utm_ref[parity, th * pair_rows:(th + 1) * pair_rows, :] = (
                sorted_rows[parity * pair_rows:(parity + 1) * pair_rows])
    for c in range(n_clusters):
        cols = slice(c * LANES, (c + 1) * LANES)
        bu_ref[c] = _dot(jnp.concatenate([utm_ref[0, :, cols], utm_ref[1, :, cols]], axis=1),
                         wv_ref[c])
        project(1)

    for c in range(n_clusters):
        lo, hi = c * CLUSTER_STATE, (c + 1) * CLUSTER_STATE
        cols = slice(c * LANES, (c + 1) * LANES)
        lam_re = lamre_ref[:, lo:hi]
        lam_im = lamim_ref[:, lo:hi]
        h_re = state_ref[0, :, lo:hi]
        h_im = state_ref[1, :, lo:hi]
        s_ref[c, 0:n_batch, 0:CLUSTER_STATE] = h_re.astype(jnp.bfloat16)
        s_ref[c, 0:n_batch, CLUSTER_STATE:2 * CLUSTER_STATE] = h_im.astype(jnp.bfloat16)
        for j in range(lt // 2):
            r0 = j * n_batch
            b_re = bu_ref[c, r0:r0 + n_batch, 0:CLUSTER_STATE]
            b_im = bu_ref[c, r0:r0 + n_batch, CLUSTER_STATE:2 * CLUSTER_STATE]
            n_re = lam_re * h_re - lam_im * h_im + b_re
            n_im = lam_re * h_im + lam_im * h_re + b_im
            h_re, h_im = n_re, n_im
            r1 = r0 + n_batch
            s_ref[c, r1:r1 + n_batch, 0:CLUSTER_STATE] = n_re.astype(jnp.bfloat16)
            s_ref[c, r1:r1 + n_batch, CLUSTER_STATE:2 * CLUSTER_STATE] = n_im.astype(jnp.bfloat16)
        state_ref[0, :, lo:hi] = h_re
        state_ref[1, :, lo:hi] = h_im
        z = _dot(s_ref[c], cz_ref[c])
        y_ref[0, :, cols] = z[:half, LANES:] + _dot(utm_ref[0, :, cols], dmat_ref[c])
        y_ref[1, :, cols] = z[n_batch:, :LANES]

    perm_t = permt_ref[...]
    n_groups = lt // PERM_T
    half_groups = n_groups // 2

    def glu_stage(g0):
        span = slice(g0 * pair_rows, (g0 + half_groups) * pair_rows)
        y = jnp.concatenate(
            [y_ref[parity, span, :] + dskip_ref[...] * utm_ref[parity, span, :].astype(jnp.float32)
             for parity in range(2)], axis=0)
        y = jax.nn.gelu(y)
        return y, _dot(y.astype(jnp.bfloat16), wglu_ref[...])

    def store_stage(g0, y, glu):
        y = y * jax.nn.sigmoid(glu + bglu_ref[...])
        yn = _rms(y, gssm_ref[...]).astype(jnp.bfloat16)
        quarter = half_groups * pair_rows
        for g in range(half_groups):
            sorted_rows = jnp.concatenate(
                [yn[parity * quarter + g * pair_rows: parity * quarter + (g + 1) * pair_rows]
                 for parity in range(2)], axis=0)
            back = _dot(perm_t, sorted_rows).astype(jnp.bfloat16)
            th = g0 + g
            for b in range(n_batch):
                yssm_ref[b, th * PERM_T:(th + 1) * PERM_T, :] = back[b * PERM_T:(b + 1) * PERM_T]

    first = glu_stage(0)
    project(len(projections))
    second = glu_stage(half_groups)
    store_stage(0, *first)
    store_stage(half_groups, *second)


def _mixer_in(x, g_pre, w_in, perm, perm_t, lam_re, lam_im, wv, cz, dmat, d_skip, w_glu, b_glu,
              g_ssm):
    n_batch, seq, d_model = x.shape
    d_ssm = w_glu.shape[0]
    n_clusters = d_ssm // LANES
    lt = TIME_TILE
    rows = n_batch * lt
    assert seq % lt == 0 and lt % PERM_T == 0 and n_batch % BF16_SUBLANES == 0
    resident = lambda a: pl.BlockSpec(a.shape, lambda i: (0,) * a.ndim, pipeline_mode=pl.Buffered(1))
    act_spec = pl.BlockSpec((n_batch, lt, d_ssm), lambda i: (0, i, 0))
    act_shape = jax.ShapeDtypeStruct((n_batch, seq, d_ssm), jnp.bfloat16)
    consts = (g_pre, w_in, perm, perm_t, lam_re, lam_im, wv, cz, dmat, d_skip, w_glu, b_glu, g_ssm)
    return pl.pallas_call(
        functools.partial(_mixer_in_kernel, n_batch=n_batch, n_clusters=n_clusters),
        out_shape=(act_shape,) * 4,
        grid=(seq // lt,),
        in_specs=[pl.BlockSpec((n_batch, lt, d_model), lambda i: (0, i, 0))]
        + [resident(a) for a in consts],
        out_specs=(act_spec,) * 4,
        scratch_shapes=[
            pltpu.VMEM((2, n_batch, n_clusters * CLUSTER_STATE), jnp.float32),
            pltpu.VMEM((2, rows // 2, d_ssm), jnp.bfloat16),
            pltpu.VMEM((n_clusters, rows // 2, 2 * CLUSTER_STATE), jnp.float32),
            pltpu.VMEM((n_clusters, rows // 2 + n_batch, 2 * CLUSTER_STATE), jnp.bfloat16),
            pltpu.VMEM((2, rows // 2, d_ssm), jnp.float32),
        ],
        compiler_params=pltpu.CompilerParams(
            dimension_semantics=("arbitrary",), vmem_limit_bytes=VMEM_LIMIT_BYTES),
        name="mixer_in",
    )(x, *consts)


def _block_order(i):
    return [i] + list(range(i))


def _score_ops(q_aug_t, ka_ref, i, causal, out):
    for j in _block_order(i):
        s = _dot(ka_ref[j * MOBA_BLOCK:(j + 1) * MOBA_BLOCK, :], q_aug_t)
        out.append(jnp.where(causal, s, MASK_NEG) if j == i else s)
        yield


def _prob_ops(q_aug_t, ka_ref, i, causal, out):
    m_own = None
    for j in _block_order(i):
        s = _dot(ka_ref[j * MOBA_BLOCK:(j + 1) * MOBA_BLOCK, :], q_aug_t)
        if j == i:
            s = jnp.where(causal, s, MASK_NEG)
            m_own = jnp.max(s, axis=0, keepdims=True)
        out.append(jnp.exp2(s - m_own).astype(jnp.bfloat16))
        yield


def _value_ops(scores, vt_ref, i, out, overflow):
    normalised = overflow is not None
    if not normalised:
        m = functools.reduce(jnp.maximum, [jnp.max(s, axis=0, keepdims=True) for s in scores])
    acc = None
    for j, s in zip(_block_order(i), scores):
        p = s if normalised else jnp.exp2(s - m).astype(jnp.bfloat16)
        part = _dot(vt_ref[:, j * MOBA_BLOCK:(j + 1) * MOBA_BLOCK], p)
        acc = part if acc is None else acc + part
        yield
    if normalised:
        overflow.append(jnp.where(jnp.isfinite(acc), 0.0, 1.0))
    out.append(acc[:HEAD_DIM] / acc[HEAD_DIM:HEAD_DIM + 1])


def _interleave(*generators):
    live = list(generators)
    while live:
        for g in list(live):
            try:
                next(g)
            except StopIteration:
                live.remove(g)


def _moba_kernel(q_ref, k_ref, v_ref, o_ref, ka_ref, vt_ref, *, n_blocks):
    slabs = []
    for s in range(q_ref.shape[1] // LANES):
        cols = slice(s * LANES, (s + 1) * LANES)
        slabs.append(_moba_slab(q_ref.at[:, cols], k_ref.at[:, cols], v_ref.at[:, cols],
                                o_ref.at[:, cols], ka_ref.at[s], vt_ref.at[s], n_blocks))

    overflow = []
    _run_stages([slab(overflow) for slab in slabs], n_blocks)
    overflowed = jnp.max(functools.reduce(jnp.maximum, overflow)) > 0.0

    @pl.when(overflowed)
    def _():
        _run_stages([slab(None) for slab in slabs], n_blocks)


def _run_stages(stages, n_blocks):
    items = [(s, i) for s in range(len(stages)) for i in range(n_blocks - 1, -1, -1)]
    gates, ranks, scores = {}, {}, {}
    for step in range(len(items) + 3):
        if step < len(items):
            s, i = items[step]
            gates[s, i] = stages[s].gate_stage(i)
        if 0 <= step - 1 < len(items):
            s, i = items[step - 1]
            ranks[s, i] = stages[s].rank_stage(i, gates.pop((s, i)))
        pending = []
        if 0 <= step - 2 < len(items):
            s, i = items[step - 2]
            scores[s, i] = ([], [])
            pending.append(stages[s].score_stage(i, ranks.pop((s, i)), scores[s, i]))
        if 0 <= step - 3 < len(items):
            s, i = items[step - 3]
            pending.append(stages[s].value_stage(i, scores.pop((s, i))))
        _interleave(*pending)


def _moba_slab(q_ref, k_ref, v_ref, o_ref, ka_ref, vt_ref, n_blocks):
    seq = q_ref.shape[0]
    n_pairs = n_blocks * n_blocks
    assert LANES == 2 * HEAD_DIM and n_pairs <= HEAD_DIM
    lane = lax.broadcasted_iota(jnp.int32, (1, LANES), 1)
    own0 = lane < HEAD_DIM
    causal = (lax.broadcasted_iota(jnp.int32, (MOBA_BLOCK, MOBA_BLOCK), 0)
              <= lax.broadcasted_iota(jnp.int32, (MOBA_BLOCK, MOBA_BLOCK), 1))

    key_blk = lax.broadcasted_iota(jnp.int32, (seq, 1), 0) // MOBA_BLOCK
    lane_full = lax.broadcasted_iota(jnp.int32, (seq, LANES), 1)
    k_all = k_ref[...]
    ka_ref[0] = jnp.where(own0, k_all,
                          jnp.where(lane_full == key_blk + HEAD_DIM, 1.0, 0.0).astype(jnp.bfloat16))
    ka_ref[1] = jnp.where(own0, jnp.where(lane_full == key_blk, 1.0, 0.0).astype(jnp.bfloat16), k_all)
    v_t = v_ref[...].T
    ones_rows = jnp.where(lax.broadcasted_iota(jnp.int32, (BF16_SUBLANES, seq), 0) == 0, 1.0, 0.0)
    ones_rows = ones_rows.astype(jnp.bfloat16)
    vt_ref[0] = jnp.concatenate([v_t[:HEAD_DIM], ones_rows], axis=0)
    vt_ref[1] = jnp.concatenate([v_t[HEAD_DIM:], ones_rows], axis=0)

    k_mean = jnp.sum(k_all.astype(jnp.float32).reshape(n_blocks, MOBA_BLOCK, LANES), axis=1)
    k_mean = k_mean * (1.0 / MOBA_BLOCK)
    km_a = jnp.broadcast_to(k_mean[:, None, :], (n_blocks, n_blocks, LANES)).reshape(n_pairs, LANES)
    km_b = jnp.broadcast_to(k_mean[None, :, :], (n_blocks, n_blocks, LANES)).reshape(n_pairs, LANES)
    pad = [jnp.zeros((HEAD_DIM - n_pairs, LANES), jnp.float32)] if n_pairs < HEAD_DIM else []

    def both_heads(km):
        return jnp.concatenate(
            [jnp.where(own0, km, 0.0)] + pad + [jnp.where(own0, 0.0, km)] + pad, axis=0)

    def split(m):
        hi = m.astype(jnp.bfloat16)
        return [hi, (m - hi.astype(jnp.float32)).astype(jnp.bfloat16)]
    gate_rhs = jnp.concatenate(split(both_heads(km_a)) + split(both_heads(km_b)), axis=0)

    trip = lane % HEAD_DIM
    trip_j, trip_jp = trip // n_blocks, trip % n_blocks
    trip_ok = trip < n_pairs
    r_idx = lax.broadcasted_iota(jnp.int32, (LANES, LANES), 0)
    c_idx = lax.broadcasted_iota(jnp.int32, (LANES, LANES), 1)
    r_bias_lane = jnp.where(r_idx < HEAD_DIM, HEAD_DIM, 0) + (r_idx % HEAD_DIM) // n_blocks
    rank_sum = jnp.where((c_idx == r_bias_lane) & (r_idx % HEAD_DIM < n_pairs), 1.0, 0.0)
    rank_sum = rank_sum.astype(jnp.bfloat16)
    bias_blk = jnp.where(own0, lane, lane - HEAD_DIM)
    bias_lane = bias_blk < n_blocks

    def gate_stage(i):
        q_i = q_ref[i * MOBA_BLOCK:(i + 1) * MOBA_BLOCK, :]
        return _dot_nt(q_i * jnp.bfloat16(HEAD_DIM ** -0.5), gate_rhs)

    def rank_stage(i, gates):
        gate_j = gates[:, 0:LANES] + gates[:, LANES:2 * LANES]
        gate_jp = gates[:, 2 * LANES:3 * LANES] + gates[:, 3 * LANES:]
        beats = (((gate_jp > gate_j) | ((gate_jp == gate_j) & (trip_jp < trip_j)))
                 & (trip_jp < i) & trip_ok)
        return _dot(jnp.where(beats, 1.0, 0.0).astype(jnp.bfloat16), rank_sum)

    def make_stages(overflow):
        single_pass = overflow is not None

        def score_stage(i, rank, out):
            keep = ((rank < MOBA_TOPK) & (bias_blk < i)) | (bias_blk == i) | ~bias_lane
            bias = jnp.where(keep, 0.0, MASK_NEG).astype(jnp.bfloat16)
            q_i = q_ref[i * MOBA_BLOCK:(i + 1) * MOBA_BLOCK, :]
            q_s = (q_i.astype(jnp.float32) * (HEAD_DIM ** -0.5 * LOG2_E)).astype(jnp.bfloat16)
            for hd, q_aug in enumerate((jnp.where(own0, q_s, bias), jnp.where(own0, bias, q_s))):
                if single_pass:
                    yield from _prob_ops(q_aug.T, ka_ref.at[hd], i, causal, out[hd])
                else:
                    yield from _score_ops(q_aug.T, ka_ref.at[hd], i, causal, out[hd])

        def value_stage(i, scores):
            out = []
            yield from _value_ops(scores[0], vt_ref.at[0], i, out, overflow)
            yield from _value_ops(scores[1], vt_ref.at[1], i, out, overflow)
            o_ref[i * MOBA_BLOCK:(i + 1) * MOBA_BLOCK, :] = (
                jnp.concatenate(out, axis=0).T.astype(o_ref.dtype))

        return _SlabStages(gate_stage, rank_stage, score_stage, value_stage)

    return make_stages


def _moba(q, k, v):
    n_batch, seq, d_attn = q.shape
    assert seq % MOBA_BLOCK == 0 and d_attn % LANES == 0
    n_blocks = seq // MOBA_BLOCK
    width = MOBA_SLABS * LANES
    assert d_attn % width == 0
    spec = pl.BlockSpec((None, seq, width), lambda b, s: (b, 0, s))
    return pl.pallas_call(
        functools.partial(_moba_kernel, n_blocks=n_blocks),
        out_shape=jax.ShapeDtypeStruct((n_batch, seq, d_attn), jnp.bfloat16),
        grid=(n_batch, d_attn // width),
        in_specs=[spec, spec, spec],
        out_specs=spec,
        scratch_shapes=[
            pltpu.VMEM((MOBA_SLABS, LANES // HEAD_DIM, seq, LANES), jnp.bfloat16),
            pltpu.VMEM((MOBA_SLABS, LANES // HEAD_DIM, HEAD_DIM + BF16_SUBLANES, seq),
                       jnp.bfloat16)],
        compiler_params=pltpu.CompilerParams(
            dimension_semantics=("arbitrary", "arbitrary"), vmem_limit_bytes=VMEM_LIMIT_BYTES),
        name="moba",
    )(q, k, v)


def _out_ffn_kernel(x_ref, ys_ref, ya_ref, gattn_ref, wo_ref, gpost_ref, gpre_ref,
                    wg_ref, wu_ref, wd_ref, gffn_ref, o_ref):
    n_chunks = wg_ref.shape[1] // FF_CHUNK
    part_rows = x_ref.shape[0] // ROW_PARTS
    d_ssm = ys_ref.shape[1]

    def mix_stage(r):
        rows = slice(r * part_rows, (r + 1) * part_rows)
        ya = _rms(ya_ref[rows, :].astype(jnp.float32), gattn_ref[...]).astype(jnp.bfloat16)
        mixed = _dot(ys_ref[rows, :], wo_ref[:d_ssm, :]) + _dot(ya, wo_ref[d_ssm:, :])
        x1 = x_ref[rows, :] + _rms(mixed, gpost_ref[...])
        return x1, _rms(x1, gpre_ref[...]).astype(jnp.bfloat16)

    def gate_up(h, c):
        lo, hi = c * FF_CHUNK, (c + 1) * FF_CHUNK
        return _dot(h, wg_ref[:, lo:hi]), _dot(h, wu_ref[:, lo:hi])

    def ffn_chunk(h, c, pending, acc):
        gate, up = pending
        pending = gate_up(h, c + 1) if c + 1 < n_chunks else None
        f = (jax.nn.silu(gate) * up).astype(jnp.bfloat16)
        part = _dot(f, wd_ref[c * FF_CHUNK:(c + 1) * FF_CHUNK, :])
        return pending, part if acc is None else acc + part

    def out_stage(r, x1, acc):
        o_ref[r * part_rows:(r + 1) * part_rows, :] = x1 + _rms(acc, gffn_ref[...])

    parts = [mix_stage(r) for r in range(ROW_PARTS)]
    done = None
    for r, (_, h) in enumerate(parts):
        pending, acc = gate_up(h, 0), None
        for c in range(n_chunks):
            pending, acc = ffn_chunk(h, c, pending, acc)
            if c == 0 and r > 0:
                out_stage(r - 1, parts[r - 1][0], done)
        done = acc
    out_stage(ROW_PARTS - 1, parts[-1][0], done)


def _out_ffn(x2, ys2, ya2, g_attn, wo, g_post, g_pre, wg, wu, wd, g_ffn):
    n_rows, d_model = x2.shape
    d_half = ys2.shape[1]
    d_ff = wg.shape[1]
    assert n_rows % ROW_TILE == 0 and d_ff % FF_CHUNK == 0
    row = lambda i: (i, 0)
    resident = lambda a: pl.BlockSpec(a.shape, lambda i: (0,) * a.ndim, pipeline_mode=pl.Buffered(1))
    consts = (g_attn, wo, g_post, g_pre, wg, wu, wd, g_ffn)
    return pl.pallas_call(
        _out_ffn_kernel,
        out_shape=jax.ShapeDtypeStruct((n_rows, d_model), jnp.float32),
        grid=(n_rows // ROW_TILE,),
        in_specs=[
            pl.BlockSpec((ROW_TILE, d_model), row),
            pl.BlockSpec((ROW_TILE, d_half), row),
            pl.BlockSpec((ROW_TILE, d_half), row),
        ] + [resident(a) for a in consts],
        out_specs=pl.BlockSpec((ROW_TILE, d_model), row),
        compiler_params=pltpu.CompilerParams(
            dimension_semantics=("arbitrary",), vmem_limit_bytes=VMEM_LIMIT_BYTES),
        name="out_ffn",
    )(x2, ys2, ya2, *consts)


def _s5_params(a_re, a_im, log_dt, b_re, b_im, c_re, c_im, n_batch):
    f32 = jnp.float32
    a_re, a_im = a_re.astype(f32), a_im.astype(f32)
    b_re, b_im = b_re.astype(f32), b_im.astype(f32)
    n_groups = a_re.shape[0]
    n_clusters = n_groups // GROUPS_PER_CLUSTER
    dt = jnp.exp(log_dt.astype(f32))[:, None]
    mag = jnp.exp(a_re * dt)
    ang = a_im * dt
    lb_re = mag * jnp.cos(ang)
    lb_im = mag * jnp.sin(ang)
    nr = lb_re - 1.0
    den = a_re * a_re + a_im * a_im
    cr = (nr * a_re + lb_im * a_im) / den
    ci = (lb_im * a_re - nr * a_im) / den
    bb_re = cr[..., None] * b_re - ci[..., None] * b_im
    bb_im = cr[..., None] * b_im + ci[..., None] * b_re
    eye = jnp.eye(GROUPS_PER_CLUSTER, dtype=f32)

    def in_proj(bb):
        bb = bb.reshape(n_clusters, GROUPS_PER_CLUSTER, SSM_STATE, SSM_GROUP)
        m = jnp.einsum('cgph,gk->cghkp', bb, eye)
        return m.reshape(n_clusters, LANES, CLUSTER_STATE)

    def out_proj(cc):
        cc = cc.reshape(n_clusters, GROUPS_PER_CLUSTER, SSM_GROUP, SSM_STATE)
        m = jnp.einsum('cghp,gk->cgpkh', cc, eye)
        return m.reshape(n_clusters, CLUSTER_STATE, LANES)

    def rotate(m_re, m_im, l_re, l_im):
        return m_re * l_re - m_im * l_im, m_re * l_im + m_im * l_re

    def state_cols(m_re, m_im):
        return jnp.concatenate([in_proj(m_re), in_proj(m_im)], axis=2)

    def state_rows(m_re, m_im):
        return jnp.concatenate([out_proj(m_re), -out_proj(m_im)], axis=1)

    bl_re, bl_im = rotate(bb_re, bb_im, lb_re[..., None], lb_im[..., None])
    wv = jnp.concatenate([state_cols(bl_re, bl_im), state_cols(bb_re, bb_im)], axis=1)
    c_re, c_im = c_re.astype(f32), c_im.astype(f32)
    cl_re, cl_im = rotate(c_re, c_im, lb_re[:, None, :], lb_im[:, None, :])
    cz = jnp.concatenate([state_rows(c_re, c_im), state_rows(cl_re, cl_im)], axis=2)
    direct = (jnp.einsum('gop,gpi->gio', c_re, bb_re) - jnp.einsum('gop,gpi->gio', c_im, bb_im))
    direct = direct.reshape(n_clusters, GROUPS_PER_CLUSTER, SSM_GROUP, SSM_GROUP)
    dmat = jnp.einsum('cgio,gk->cgiko', direct, eye).reshape(n_clusters, LANES, LANES)
    l2_re, l2_im = rotate(lb_re, lb_im, lb_re, lb_im)
    lam2_re = jnp.broadcast_to(l2_re.reshape(1, -1), (n_batch, l2_re.size))
    lam2_im = jnp.broadcast_to(l2_im.reshape(1, -1), (n_batch, l2_im.size))
    bf16 = jnp.bfloat16
    return lam2_re, lam2_im, wv.astype(bf16), cz.astype(bf16), dmat.astype(bf16)


def _time_batch_permutation(n_batch):
    n = n_batch * PERM_T
    r = jnp.arange(n)
    parity, rest = r // (n // 2), r % (n // 2)
    src = (rest % n_batch) * PERM_T + 2 * (rest // n_batch) + parity
    return (src[:, None] == jnp.arange(n)[None, :]).astype(jnp.bfloat16)


def _layer(x, g_pre_mix, w_in, ssm_a_re, ssm_a_im, ssm_log_dt, ssm_b_re, ssm_b_im, ssm_c_re,
           ssm_c_im, ssm_d, w_glu, b_glu, g_ssm_out, g_attn_out, w_out, g_post_mix, g_pre_ffn,
           w_gate, w_up, w_down, g_post_ffn):
    n_batch, seq, d_model = x.shape
    d_ssm = w_glu.shape[0]
    bf16 = jnp.bfloat16
    row = lambda g: g.reshape(1, -1).astype(jnp.float32)
    lam_re, lam_im, wv, cz, dmat = _s5_params(ssm_a_re, ssm_a_im, ssm_log_dt, ssm_b_re, ssm_b_im,
                                              ssm_c_re, ssm_c_im, n_batch)
    perm = _time_batch_permutation(n_batch)
    q, k, v, y_ssm = _mixer_in(
        x, row(g_pre_mix), w_in, perm, perm.T, lam_re, lam_im, wv, cz, dmat, row(ssm_d), w_glu,
        row(b_glu), row(g_ssm_out))
    y_attn = _moba(q, k, v)
    n_rows = n_batch * seq
    out = _out_ffn(
        x.reshape(n_rows, d_model), y_ssm.reshape(n_rows, d_ssm), y_attn.reshape(n_rows, -1),
        row(g_attn_out), w_out.astype(bf16), row(g_post_mix), row(g_pre_ffn), w_gate.astype(bf16),
        w_up.astype(bf16), w_down.astype(bf16), row(g_post_ffn))
    return out.reshape(n_batch, seq, d_model)


def kernel(x, g_pre_mix, w_in, ssm_a_re, ssm_a_im, ssm_log_dt, ssm_b_re, ssm_b_im, ssm_c_re, ssm_c_im, ssm_d, w_glu, b_glu, g_ssm_out, g_attn_out, w_out, g_post_mix, g_pre_ffn, w_gate, w_up, w_down, g_post_ffn):
    for l in range(w_in.shape[0]):
        x = _layer(x, g_pre_mix[l], w_in[l], ssm_a_re[l], ssm_a_im[l], ssm_log_dt[l], ssm_b_re[l],
                   ssm_b_im[l], ssm_c_re[l], ssm_c_im[l], ssm_d[l], w_glu[l], b_glu[l],
                   g_ssm_out[l], g_attn_out[l], w_out[l], g_post_mix[l], g_pre_ffn[l], w_gate[l],
                   w_up[l], w_down[l], g_post_ffn[l])
    return x
```
